```python
import jax
import jax.numpy as jnp
from jax import lax
import numpy as np

D_MODEL = 4096
BATCH = 4
SEQ = 2048
DEPTH = 2
DEC_BATCH = 16
DEC_SEQ = 16
PAST_LEN = 1024

CHUNK = 64
N_META = 16
NORM_EPS = 1e-6
CONV_W = 4

S5_GROUP = 16
S5_DIM = D_MODEL // 2
S5_GROUPS = S5_DIM // S5_GROUP
S5_STATE = 64
MB_HEADDIM = 64
MB_DIM = D_MODEL
MB_HEADS = MB_DIM // MB_HEADDIM
MB_GROUPS = 8
MB_STATE = 128
MB_CONV_DIM = MB_DIM + 2 * MB_GROUPS * MB_STATE
MIX0_IN = S5_DIM + MB_DIM + MB_CONV_DIM + MB_HEADS
MIX0_OUT = S5_DIM + MB_DIM

GDN_QK_HEADS = 32
GDN_V_HEADS = 64
GDN_DK = 128
GDN_DV = 128
GDN_QK_DIM = GDN_QK_HEADS * GDN_DK
GDN_V_DIM = GDN_V_HEADS * GDN_DV
GDN_CONV_DIM = 2 * GDN_QK_DIM + GDN_V_DIM
GDN_IN = GDN_CONV_DIM + GDN_V_DIM + 2 * GDN_V_HEADS

D_FF = 14336
N_EXPERTS = 8
TOP_K = 2
D_FF_EXPERT = 14336

kernel_name = 'hybrid_s5_ssd_gdn_stream_step'


def rms_norm(x, w):
    xf = x.astype(jnp.float32)
    y = xf * lax.rsqrt(jnp.mean(xf * xf, axis=-1, keepdims=True) + NORM_EPS)
    return (y * w.astype(jnp.float32)).astype(x.dtype)


def l2_normalize(x):
    return x * lax.rsqrt(jnp.sum(x * x, axis=-1, keepdims=True) + 1e-6)


def causal_conv(x, prev, w, b=None):
    L = x.shape[1]
    xp = jnp.concatenate([prev.astype(x.dtype), x], axis=1)
    out = xp[:, 0:L] * w[0]
    for j in range(1, CONV_W):
        out = out + xp[:, j:j + L] * w[j]
    if b is not None:
        out = out + b
    return out, xp[:, L:]


def _to_chunks(a, chunk):
    b, L = a.shape[:2]
    return jnp.moveaxis(a.reshape((b, L // chunk, chunk) + a.shape[2:]), 1, 0)


def _from_chunks(a):
    nc, b, q = a.shape[:3]
    return jnp.moveaxis(a, 0, 1).reshape((b, nc * q) + a.shape[3:])


def segmented_scan(body, xs, carry, segments):
    outs = []
    for start, length, chunk in segments:
        seg = tuple(lax.slice_in_dim(a, start, start + length, axis=1) for a in xs)
        carry, ys = lax.scan(body, carry, tuple(_to_chunks(a, chunk) for a in seg))
        outs.append(_from_chunks(ys))
    y = outs[0] if len(outs) == 1 else jnp.concatenate(outs, axis=1)
    return y, carry


def s5_mixer(u, h0_re, h0_im, segments, lam_re, lam_im, log_step, b_re, b_im, c_re, c_im):
    bsz, L, _ = u.shape
    lam_re = lam_re.astype(jnp.float32)
    lam_im = lam_im.astype(jnp.float32)
    step = jnp.exp(log_step.astype(jnp.float32))[:, None]
    mag = jnp.exp(lam_re * step)
    a_re = mag * jnp.cos(lam_im * step)
    a_im = mag * jnp.sin(lam_im * step)
    den = lam_re * lam_re + lam_im * lam_im
    f_re = ((a_re - 1.0) * lam_re + a_im * lam_im) / den
    f_im = (a_im * lam_re - (a_re - 1.0) * lam_im) / den
    b_re = b_re.astype(jnp.float32)
    b_im = b_im.astype(jnp.float32)
    c_re = c_re.astype(jnp.float32)
    c_im = c_im.astype(jnp.float32)
    bb_re = f_re[..., None] * b_re - f_im[..., None] * b_im
    bb_im = f_re[..., None] * b_im + f_im[..., None] * b_re

    def combine(e1, e2):
        a1r, a1i, s1r, s1i = e1
        a2r, a2i, s2r, s2i = e2
        return (a2r * a1r - a2i * a1i, a2r * a1i + a2i * a1r,
                a2r * s1r - a2i * s1i + s2r, a2r * s1i + a2i * s1r + s2i)

    def body(carry, inp):
        h_re, h_im = carry
        (u_c,) = inp
        bu_re = jnp.einsum('gpj,bqgj->bqgp', bb_re, u_c)
        bu_im = jnp.einsum('gpj,bqgj->bqgp', bb_im, u_c)
        pw_re, pw_im, s_re, s_im = lax.associative_scan(
            combine,
            (jnp.broadcast_to(a_re, bu_re.shape), jnp.broadcast_to(a_im, bu_im.shape), bu_re, bu_im),
            axis=1)
        hr = s_re + pw_re * h_re[:, None] - pw_im * h_im[:, None]
        hi = s_im + pw_re * h_im[:, None] + pw_im * h_re[:, None]
        y = jnp.einsum('gjp,bqgp->bqgj', c_re, hr) - jnp.einsum('gjp,bqgp->bqgj', c_im, hi)
        return (hr[:, -1], hi[:, -1]), y

    ug = u.reshape(bsz, L, S5_GROUPS, S5_GROUP)
    y, (h_re, h_im) = segmented_scan(body, (ug,), (h0_re.astype(jnp.float32), h0_im.astype(jnp.float32)), segments)
    return y.reshape(bsz, L, S5_DIM), h_re, h_im


def ssd_mixer(xm, dt, bm, cm, h0, segments, a_log, d_skip):
    R = MB_HEADS // MB_GROUPS
    a_neg = -jnp.exp(a_log.astype(jnp.float32)).reshape(MB_GROUPS, R)
    d = d_skip.astype(jnp.float32).reshape(MB_GROUPS, R)

    def body(h, inp):
        x, dtc, bc, cc = inp
        bsz, Q = x.shape[:2]
        x = x.reshape(bsz, Q, MB_GROUPS, R, MB_HEADDIM)
        dtc = dtc.reshape(bsz, Q, MB_GROUPS, R)
        cum = jnp.cumsum(dtc * a_neg, axis=1)
        idx = jnp.arange(Q)
        causal = (idx[:, None] >= idx[None, :])[None, :, :, None, None]
        seg = jnp.where(causal, cum[:, :, None] - cum[:, None, :], -jnp.inf)
        cb = jnp.einsum('bqgn,bkgn->bqkg', cc, bc)
        m = cb[..., None] * jnp.exp(seg) * dtc[:, None]
        y = jnp.einsum('bqkgr,bkgrp->bqgrp', m, x)
        y = y + jnp.einsum('bqgn,bgrpn->bqgrp', cc, h) * jnp.exp(cum)[..., None]
        y = y + d[..., None] * x
        w_end = jnp.exp(cum[:, -1:] - cum) * dtc
        h_new = h * jnp.exp(cum[:, -1])[..., None, None] + jnp.einsum('bkgn,bkgrp->bgrpn', bc, x * w_end[..., None])
        return h_new, y.reshape(bsz, Q, MB_HEADS, MB_HEADDIM)

    bsz = xm.shape[0]
    h0g = h0.astype(jnp.float32).reshape(bsz, MB_GROUPS, R, MB_HEADDIM, MB_STATE)
    y, h = segmented_scan(body, (xm, dt, bm, cm), h0g, segments)
    return y, h.reshape(bsz, MB_HEADS, MB_HEADDIM, MB_STATE)


def gdn_mixer(q, k, v, g, beta, s0, segments):
    def body(S, inp):
        qc, kc, vc, gch, bch = (jnp.moveaxis(a, 2, 1) for a in inp)
        Q = qc.shape[2]
        gc = jnp.cumsum(gch, axis=-1)
        idx = jnp.arange(Q)
        causal = idx[:, None] >= idx[None, :]
        strict = idx[:, None] > idx[None, :]
        decay = jnp.exp(jnp.where(causal, gc[..., :, None] - gc[..., None, :], -jnp.inf))
        kb = kc * bch[..., None]
        lower = jnp.where(strict, jnp.einsum('bhik,bhjk->bhij', kb, kc) * decay, 0.0)
        system = lower + jnp.eye(Q, dtype=jnp.float32)
        rhs = jnp.concatenate([vc * bch[..., None], kb * jnp.exp(gc)[..., None]], axis=-1)
        sol = lax.linalg.triangular_solve(system, rhs, left_side=True, lower=True, unit_diagonal=True)
        uu, ww = sol[..., :GDN_DV], sol[..., GDN_DV:]
        v_new = uu - jnp.einsum('bhik,bhkv->bhiv', ww, S)
        attn = jnp.where(causal, jnp.einsum('bhik,bhjk->bhij', qc, kc) * decay, 0.0)
        o = jnp.einsum('bhik,bhkv->bhiv', qc * jnp.exp(gc)[..., None], S) + jnp.einsum('bhij,bhjv->bhiv', attn, v_new)
        g_last = gc[..., -1]
        S_new = S * jnp.exp(g_last)[..., None, None] + jnp.einsum(
            'bhjk,bhjv->bhkv', kc * jnp.exp(g_last[..., None] - gc)[..., None], v_new)
        return S_new, jnp.moveaxis(o, 1, 2)

    return segmented_scan(body, (q, k, v, g, beta), s0.astype(jnp.float32), segments)


def s5_ssd_block(h, st, segments, p):
    s5_re0, s5_im0, ssd0, conv0 = st
    bsz, L, _ = h.shape
    proj = h @ p['w_in0']
    o1 = S5_DIM
    o2 = o1 + MB_DIM
    o3 = o2 + MB_CONV_DIM
    u = proj[..., :o1].astype(jnp.float32)
    z = proj[..., o1:o2].astype(jnp.float32)
    xbc, conv_new = causal_conv(proj[..., o2:o3], conv0, p['mb_conv_w'], p['mb_conv_b'])
    dt_raw = proj[..., o3:].astype(jnp.float32)
    y_s5, s5_re, s5_im = s5_mixer(u, s5_re0, s5_im0, segments, p['s5_lambda_re'], p['s5_lambda_im'],
                                  p['s5_log_step'], p['s5_b_re'], p['s5_b_im'], p['s5_c_re'], p['s5_c_im'])
    y_s5 = y_s5 + p['s5_d'].astype(jnp.float32) * u
    gl = jax.nn.gelu(y_s5)
    y_s5 = gl * jax.nn.sigmoid(gl @ p['s5_glu_w'].astype(jnp.float32) + p['s5_glu_b'].astype(jnp.float32))
    y_s5 = rms_norm(y_s5, p['s5_norm'])
    xbc = jax.nn.silu(xbc.astype(jnp.float32))
    gn = MB_GROUPS * MB_STATE
    xm = xbc[..., :MB_DIM].reshape(bsz, L, MB_HEADS, MB_HEADDIM)
    bm = xbc[..., MB_DIM:MB_DIM + gn].reshape(bsz, L, MB_GROUPS, MB_STATE)
    cm = xbc[..., MB_DIM + gn:].reshape(bsz, L, MB_GROUPS, MB_STATE)
    dt = jax.nn.softplus(dt_raw + p['mb_dt_bias'].astype(jnp.float32))
    y_mb, ssd = ssd_mixer(xm, dt, bm, cm, ssd0, segments, p['mb_a_log'], p['mb_d'])
    y_mb = y_mb.reshape(bsz, L, MB_DIM) * jax.nn.silu(z)
    gsz = MB_DIM // MB_GROUPS
    y_mb = rms_norm(y_mb.reshape(bsz, L, MB_GROUPS, gsz), p['mb_norm'].reshape(MB_GROUPS, gsz)).reshape(bsz, L, MB_DIM)
    mixed = jnp.concatenate([y_s5, y_mb], axis=-1).astype(h.dtype) @ p['w_out0']
    return mixed, (s5_re, s5_im, ssd, conv_new)


def gdn_block(h, st, segments, p):
    s0, conv0 = st
    bsz, L, _ = h.shape
    proj = h @ p['w_in1']
    o1 = GDN_CONV_DIM
    o2 = o1 + GDN_V_DIM
    o3 = o2 + GDN_V_HEADS
    qkv, conv_new = causal_conv(proj[..., :o1], conv0, p['gdn_conv_w'])
    qkv = jax.nn.silu(qkv.astype(jnp.float32))
    z = proj[..., o1:o2].astype(jnp.float32).reshape(bsz, L, GDN_V_HEADS, GDN_DV)
    b_raw = proj[..., o2:o3].astype(jnp.float32)
    a_raw = proj[..., o3:].astype(jnp.float32)
    q = l2_normalize(qkv[..., :GDN_QK_DIM].reshape(bsz, L, GDN_QK_HEADS, GDN_DK)) * (GDN_DK ** -0.5)
    k = l2_normalize(qkv[..., GDN_QK_DIM:2 * GDN_QK_DIM].reshape(bsz, L, GDN_QK_HEADS, GDN_DK))
    v = qkv[..., 2 * GDN_QK_DIM:].reshape(bsz, L, GDN_V_HEADS, GDN_DV)
    rep = GDN_V_HEADS // GDN_QK_HEADS
    q = jnp.repeat(q, rep, axis=2)
    k = jnp.repeat(k, rep, axis=2)
    beta = jax.nn.sigmoid(b_raw)
    g = -jnp.exp(p['gdn_a_log'].astype(jnp.float32)) * jax.nn.softplus(a_raw + p['gdn_dt_bias'].astype(jnp.float32))
    o, S = gdn_mixer(q, k, v, g, beta, s0, segments)
    o = rms_norm(o, p['gdn_norm']) * jax.nn.silu(z)
    mixed = o.reshape(bsz, L, GDN_V_DIM).astype(h.dtype) @ p['w_out1']
    return mixed, (S, conv_new)


def swiglu(h, wg, wu, wd):
    return (jax.nn.silu(h @ wg) * (h @ wu)) @ wd


def moe_swiglu(h, w_router, wg, wu, wd):
    logits = (h @ w_router).astype(jnp.float32)
    top_v, top_i = lax.top_k(logits, TOP_K)
    top_w = jax.nn.softmax(top_v, axis=-1)
    gates = jnp.sum(jax.nn.one_hot(top_i, N_EXPERTS, dtype=jnp.float32) * top_w[..., None], axis=-2)
    out = jnp.zeros_like(h)
    for e in range(N_EXPERTS):
        out = out + gates[..., e:e + 1].astype(h.dtype) * swiglu(h, wg[e], wu[e], wd[e])
    return out


def trunk(x, st, segments, p):
    s5_re, s5_im, ssd, ssd_conv, gdn, gdn_conv = st
    for layer in range(DEPTH):
        if layer % 2 == 0:
            mixed, (s5_re, s5_im, ssd, ssd_conv) = s5_ssd_block(
                rms_norm(x, p['norm_mix0']), (s5_re, s5_im, ssd, ssd_conv), segments, p)
            x = x + mixed
            x = x + swiglu(rms_norm(x, p['norm_ffn0']), p['ffn_w_gate'], p['ffn_w_up'], p['ffn_w_down'])
        else:
            mixed, (gdn, gdn_conv) = gdn_block(rms_norm(x, p['norm_mix1']), (gdn, gdn_conv), segments, p)
            x = x + mixed
            x = x + moe_swiglu(rms_norm(x, p['norm_ffn1']), p['moe_router'], p['moe_w_gate'],
                               p['moe_w_up'], p['moe_w_down'])
    return rms_norm(x, p['norm_final']), (s5_re, s5_im, ssd, ssd_conv, gdn, gdn_conv)


def setup_inputs(seed: int = 0) -> dict:
    key = jax.random.key(seed)
    keys = jax.random.split(key, 64)
    counter = [0]

    def nk():
        k = keys[counter[0]]
        counter[0] += 1
        return k

    def normal(shape, scale):
        return jax.random.normal(nk(), shape, jnp.float32) * scale

    def gain(n):
        return 1.0 + normal((n,), 0.02)

    def unif(shape, lo, hi):
        return jax.random.uniform(nk(), shape, jnp.float32, lo, hi)

    def inv_softplus(y):
        return jnp.log(jnp.expm1(y))

    n_idx = jnp.arange(S5_STATE, dtype=jnp.float32)
    return {
        'x_prompt': normal((BATCH, SEQ, D_MODEL), 1.0),
        'x_sample': normal((DEC_BATCH, DEC_SEQ, D_MODEL), 1.0),
        'state_s5_re': normal((DEC_BATCH, S5_GROUPS, S5_STATE), 0.1),
        'state_s5_im': normal((DEC_BATCH, S5_GROUPS, S5_STATE), 0.1),
        'state_ssd': normal((DEC_BATCH, MB_HEADS, MB_HEADDIM, MB_STATE), 0.1),
        'cache_ssd_conv': normal((DEC_BATCH, CONV_W - 1, MB_CONV_DIM), 1.0),
        'state_gdn': normal((DEC_BATCH, GDN_V_HEADS, GDN_DK, GDN_DV), 0.1),
        'cache_gdn_conv': normal((DEC_BATCH, CONV_W - 1, GDN_CONV_DIM), 1.0),
        'meta_tokens': normal((N_META, D_MODEL), 1.0),
        'norm_mix0': gain(D_MODEL),
        'w_in0': normal((D_MODEL, MIX0_IN), D_MODEL ** -0.5),
        's5_lambda_re': -0.5 + normal((S5_GROUPS, S5_STATE), 0.01),
        's5_lambda_im': np.pi * n_idx[None, :] + normal((S5_GROUPS, S5_STATE), 0.01),
        's5_log_step': jnp.log(unif((S5_GROUPS,), 0.001, 0.1)),
        's5_b_re': normal((S5_GROUPS, S5_STATE, S5_GROUP), (2 * S5_GROUP) ** -0.5),
        's5_b_im': normal((S5_GROUPS, S5_STATE, S5_GROUP), (2 * S5_GROUP) ** -0.5),
        's5_c_re': normal((S5_GROUPS, S5_GROUP, S5_STATE), (2 * S5_STATE) ** -0.5),
        's5_c_im': normal((S5_GROUPS, S5_GROUP, S5_STATE), (2 * S5_STATE) ** -0.5),
        's5_d': normal((S5_DIM,), 1.0),
        's5_glu_w': normal((S5_DIM, S5_DIM), S5_DIM ** -0.5),
        's5_glu_b': normal((S5_DIM,), 0.01),
        's5_norm': gain(S5_DIM),
        'mb_conv_w': normal((CONV_W, MB_CONV_DIM), 0.5),
        'mb_conv_b': normal((MB_CONV_DIM,), 0.01),
        'mb_dt_bias': inv_softplus(unif((MB_HEADS,), 0.001, 0.1)),
        'mb_a_log': jnp.log(unif((MB_HEADS,), 1.0, 16.0)),
        'mb_d': 1.0 + normal((MB_HEADS,), 0.1),
        'mb_norm': gain(MB_DIM),
        'w_out0': normal((MIX0_OUT, D_MODEL), MIX0_OUT ** -0.5),
        'norm_ffn0': gain(D_MODEL),
        'ffn_w_gate': normal((D_MODEL, D_FF), D_MODEL ** -0.5),
        'ffn_w_up': normal((D_MODEL, D_FF), D_MODEL ** -0.5),
        'ffn_w_down': normal((D_FF, D_MODEL), D_FF ** -0.5),
        'norm_mix1': gain(D_MODEL),
        'w_in1': normal((D_MODEL, GDN_IN), D_MODEL ** -0.5),
        'gdn_conv_w': normal((CONV_W, GDN_CONV_DIM), 0.5),
        'gdn_a_log': jnp.log(unif((GDN_V_HEADS,), 1.0, 16.0)),
        'gdn_dt_bias': inv_softplus(unif((GDN_V_HEADS,), 0.001, 0.1)),
        'gdn_norm': gain(GDN_DV),
        'w_out1': normal((GDN_V_DIM, D_MODEL), GDN_V_DIM ** -0.5),
        'norm_ffn1': gain(D_MODEL),
        'moe_router': normal((D_MODEL, N_EXPERTS), D_MODEL ** -0.5),
        'moe_w_gate': normal((N_EXPERTS, D_MODEL, D_FF_EXPERT), D_MODEL ** -0.5),
        'moe_w_up': normal((N_EXPERTS, D_MODEL, D_FF_EXPERT), D_MODEL ** -0.5),
        'moe_w_down': normal((N_EXPERTS, D_FF_EXPERT, D_MODEL), D_FF_EXPERT ** -0.5),
        'norm_final': gain(D_MODEL),
    }


def reference(x_prompt, x_sample, state_s5_re, state_s5_im, state_ssd, cache_ssd_conv, state_gdn,
              cache_gdn_conv, meta_tokens, norm_mix0, w_in0, s5_lambda_re, s5_lambda_im, s5_log_step,
              s5_b_re, s5_b_im, s5_c_re, s5_c_im, s5_d, s5_glu_w, s5_glu_b, s5_norm, mb_conv_w, mb_conv_b,
              mb_dt_bias, mb_a_log, mb_d, mb_norm, w_out0, norm_ffn0, ffn_w_gate, ffn_w_up, ffn_w_down,
              norm_mix1, w_in1, gdn_conv_w, gdn_a_log, gdn_dt_bias, gdn_norm, w_out1, norm_ffn1,
              moe_router, moe_w_gate, moe_w_up, moe_w_down, norm_final):
    p = dict(norm_mix0=norm_mix0, w_in0=w_in0, s5_lambda_re=s5_lambda_re, s5_lambda_im=s5_lambda_im,
             s5_log_step=s5_log_step, s5_b_re=s5_b_re, s5_b_im=s5_b_im, s5_c_re=s5_c_re, s5_c_im=s5_c_im,
             s5_d=s5_d, s5_glu_w=s5_glu_w, s5_glu_b=s5_glu_b, s5_norm=s5_norm, mb_conv_w=mb_conv_w,
             mb_conv_b=mb_conv_b, mb_dt_bias=mb_dt_bias, mb_a_log=mb_a_log, mb_d=mb_d, mb_norm=mb_norm,
             w_out0=w_out0, norm_ffn0=norm_ffn0, ffn_w_gate=ffn_w_gate, ffn_w_up=ffn_w_up,
             ffn_w_down=ffn_w_down, norm_mix1=norm_mix1, w_in1=w_in1, gdn_conv_w=gdn_conv_w,
             gdn_a_log=gdn_a_log, gdn_dt_bias=gdn_dt_bias, gdn_norm=gdn_norm, w_out1=w_out1,
             norm_ffn1=norm_ffn1, moe_router=moe_router, moe_w_gate=moe_w_gate, moe_w_up=moe_w_up,
             moe_w_down=moe_w_down, norm_final=norm_final)

    bsz, seq, _ = x_prompt.shape
    dt_ = x_prompt.dtype
    meta = jnp.broadcast_to(meta_tokens.astype(dt_)[None], (bsz, N_META, D_MODEL))
    xp = jnp.concatenate([meta, x_prompt], axis=1)
    zero_state = (jnp.zeros((bsz, S5_GROUPS, S5_STATE), jnp.float32),
                  jnp.zeros((bsz, S5_GROUPS, S5_STATE), jnp.float32),
                  jnp.zeros((bsz, MB_HEADS, MB_HEADDIM, MB_STATE), jnp.float32),
                  jnp.zeros((bsz, CONV_W - 1, MB_CONV_DIM), dt_),
                  jnp.zeros((bsz, GDN_V_HEADS, GDN_DK, GDN_DV), jnp.float32),
                  jnp.zeros((bsz, CONV_W - 1, GDN_CONV_DIM), dt_))
    prompt_segments = ((0, N_META, N_META), (N_META, seq, CHUNK))
    y_full, (p_s5_re, p_s5_im, p_ssd, p_ssd_conv, p_gdn, p_gdn_conv) = trunk(xp, zero_state, prompt_segments, p)
    y_prompt = y_full[:, N_META:]

    dec_len = x_sample.shape[1]
    sample_segments = ((0, dec_len, dec_len),)
    y_sample, (s_s5_re, s_s5_im, s_ssd, s_ssd_conv, s_gdn, s_gdn_conv) = trunk(
        x_sample, (state_s5_re, state_s5_im, state_ssd, cache_ssd_conv, state_gdn, cache_gdn_conv),
        sample_segments, p)

    return (y_prompt, y_sample, p_s5_re, p_s5_im, p_ssd, p_ssd_conv, p_gdn, p_gdn_conv,
            s_s5_re, s_s5_im, s_ssd, s_ssd_conv, s_gdn, s_gdn_conv)
```

```python
import functools

import jax
import jax.numpy as jnp
from jax import lax
from jax.experimental import pallas as pl
from jax.experimental.pallas import tpu as pltpu

D_MODEL = 4096
CHUNK = 64
N_META = 16
NORM_EPS = 1e-6
CONV_W = 4

S5_GROUP = 16
S5_DIM = D_MODEL // 2
S5_GROUPS = S5_DIM // S5_GROUP
S5_STATE = 64
MB_HEADDIM = 64
MB_DIM = D_MODEL
MB_HEADS = MB_DIM // MB_HEADDIM
MB_GROUPS = 8
MB_STATE = 128
MB_CONV_DIM = MB_DIM + 2 * MB_GROUPS * MB_STATE
MIX0_IN = S5_DIM + MB_DIM + MB_CONV_DIM + MB_HEADS

GDN_QK_HEADS = 32
GDN_V_HEADS = 64
GDN_DK = 128
GDN_DV = 128
GDN_QK_DIM = GDN_QK_HEADS * GDN_DK
GDN_V_DIM = GDN_V_HEADS * GDN_DV
GDN_CONV_DIM = 2 * GDN_QK_DIM + GDN_V_DIM
GDN_IN = GDN_CONV_DIM + GDN_V_DIM + 2 * GDN_V_HEADS

N_EXPERTS = 8
TOP_K = 2

V7X_VMEM_LIMIT_BYTES = 56 * 1024 * 1024
MXU_COLS = 256
ROW_TILE = 224
MM_ROWS = 1216
FFN_ROWS = 1024
FFN_SUB = 256
FFN_TF = MXU_COLS


def _cparams(*sem):
    return pltpu.CompilerParams(dimension_semantics=sem, vmem_limit_bytes=V7X_VMEM_LIMIT_BYTES)


def _rms(x, w):
    ms = jnp.mean(x * x, axis=-1, keepdims=True)
    return (x * lax.rsqrt(ms + NORM_EPS)) * w


def _norm_kernel(x_ref, w_ref, h_ref):
    h_ref[...] = _rms(x_ref[...], w_ref[...]).astype(h_ref.dtype)


def _add_norm_kernel(x_ref, d_ref, w_ref, xo_ref, h_ref):
    x = x_ref[...] + d_ref[...]
    xo_ref[...] = x
    h_ref[...] = _rms(x, w_ref[...]).astype(h_ref.dtype)


def _add_norm_only_kernel(x_ref, d_ref, w_ref, h_ref):
    h_ref[...] = _rms(x_ref[...] + d_ref[...], w_ref[...]).astype(h_ref.dtype)


def _norm_router_kernel(x_ref, w_ref, wr_ref, h_ref, lg_ref):
    h = _rms(x_ref[...], w_ref[...])
    h_ref[...] = h.astype(h_ref.dtype)
    lg_ref[...] = jnp.dot(h, wr_ref[...], preferred_element_type=jnp.float32,
                          precision=lax.Precision.HIGHEST)


def _row_spec(cols):
    return pl.BlockSpec((ROW_TILE, cols), lambda i: (i, 0))


def _full_spec(shape):
    return pl.BlockSpec(shape, lambda i: (0,) * len(shape))


def rms_norm_rows(x, w, out_dtype):
    m, d = x.shape
    return pl.pallas_call(
        _norm_kernel,
        out_shape=jax.ShapeDtypeStruct((m, d), out_dtype),
        grid=(m // ROW_TILE,),
        in_specs=[_row_spec(d), _full_spec((1, d))],
        out_specs=_row_spec(d),
        compiler_params=_cparams("parallel"),
    )(x, w.reshape(1, d))


def add_rms_norm_rows(x, delta, w):
    m, d = x.shape
    return pl.pallas_call(
        _add_norm_kernel,
        out_shape=(jax.ShapeDtypeStruct((m, d), jnp.float32), jax.ShapeDtypeStruct((m, d), jnp.bfloat16)),
        grid=(m // ROW_TILE,),
        in_specs=[_row_spec(d), _row_spec(d), _full_spec((1, d))],
        out_specs=(_row_spec(d), _row_spec(d)),
        compiler_params=_cparams("parallel"),
    )(x, delta, w.reshape(1, d))


def add_rms_norm_only_rows(x, delta, w, out_dtype):
    m, d = x.shape
    return pl.pallas_call(
        _add_norm_only_kernel,
        out_shape=jax.ShapeDtypeStruct((m, d), out_dtype),
        grid=(m // ROW_TILE,),
        in_specs=[_row_spec(d), _row_spec(d), _full_spec((1, d))],
        out_specs=_row_spec(d),
        compiler_params=_cparams("parallel"),
    )(x, delta, w.reshape(1, d))


def rms_norm_router_rows(x, w, w_router):
    m, d = x.shape
    e = w_router.shape[1]
    return pl.pallas_call(
        _norm_router_kernel,
        out_shape=(jax.ShapeDtypeStruct((m, d), jnp.bfloat16), jax.ShapeDtypeStruct((m, e), jnp.float32)),
        grid=(m // ROW_TILE,),
        in_specs=[_row_spec(d), _full_spec((1, d)), _full_spec((d, e))],
        out_specs=(_row_spec(d), _row_spec(e)),
        compiler_params=_cparams("parallel"),
    )(x, w.reshape(1, d), w_router)


def _mm_kernel(x_ref, w_ref, o_ref):
    o_ref[...] = jnp.dot(x_ref[...], w_ref[...].astype(jnp.bfloat16),
                         preferred_element_type=jnp.float32).astype(o_ref.dtype)


def _mm_res_kernel(x_ref, w_ref, r_ref, o_ref):
    o_ref[...] = r_ref[...] + jnp.dot(x_ref[...], w_ref[...].astype(jnp.bfloat16),
                                      preferred_element_type=jnp.float32)


def matmul_rows(x, w, *, col0=0, ncols=None, tn=512, residual=None, out_dtype=jnp.float32):
    m, k = x.shape
    ncols = w.shape[1] - col0 if ncols is None else ncols
    if ncols < tn:
        tn = ncols
    assert m % MM_ROWS == 0 and ncols % tn == 0 and col0 % tn == 0
    cb = col0 // tn
    x_spec = pl.BlockSpec((MM_ROWS, k), lambda i, j: (i, 0), pipeline_mode=pl.Buffered(1))
    w_spec = pl.BlockSpec((k, tn), lambda i, j: (0, j + cb))
    o_spec = pl.BlockSpec((MM_ROWS, tn), lambda i, j: (i, j))
    grid = (m // MM_ROWS, ncols // tn)
    if residual is None:
        return pl.pallas_call(
            _mm_kernel, out_shape=jax.ShapeDtypeStruct((m, ncols), out_dtype), grid=grid,
            in_specs=[x_spec, w_spec], out_specs=o_spec,
            compiler_params=_cparams("parallel", "arbitrary"))(x, w)
    return pl.pallas_call(
        _mm_res_kernel, out_shape=jax.ShapeDtypeStruct((m, ncols), jnp.float32), grid=grid,
        in_specs=[x_spec, w_spec, o_spec], out_specs=o_spec,
        compiler_params=_cparams("parallel", "arbitrary"))(x, w, residual)


def _swiglu_kernel(ce_ref, cv_ref, x_ref, wg_ref, wu_ref, wd_ref, o_ref):
    c = pl.program_id(0)
    f = pl.program_id(1)
    n_valid = cv_ref[c]

    @pl.when(f == 0)
    def _():
        o_ref[...] = jnp.zeros_like(o_ref)

    @pl.when(n_valid > 0)
    def _():
        wg = wg_ref[...].astype(jnp.bfloat16)
        wu = wu_ref[...].astype(jnp.bfloat16)
        wd = wd_ref[...].astype(jnp.bfloat16)
        for s in range(FFN_ROWS // FFN_SUB):
            @pl.when(s < n_valid)
            def _():
                rows = pl.ds(s * FFN_SUB, FFN_SUB)
                xs = x_ref[rows, :]
                g = jnp.dot(xs, wg, preferred_element_type=jnp.float32)
                u = jnp.dot(xs, wu, preferred_element_type=jnp.float32)
                h = (g * jax.nn.sigmoid(g) * u).astype(jnp.bfloat16)
                o_ref[rows, :] += jnp.dot(h, wd, preferred_element_type=jnp.float32)


def swiglu_chunks(xs, wg, wu, wd, chunk_expert, chunk_valid):
    rows, d = xs.shape
    n_chunks = rows // FFN_ROWS
    ff = wg.shape[2]
    nf = ff // FFN_TF

    def f_idx(c, f, cv):
        return jnp.where(cv[c] > 0, f, nf - 1)

    grid_spec = pltpu.PrefetchScalarGridSpec(
        num_scalar_prefetch=2,
        grid=(n_chunks, nf),
        in_specs=[
            pl.BlockSpec((FFN_ROWS, d), lambda c, f, ce, cv: (c, 0), pipeline_mode=pl.Buffered(1)),
            pl.BlockSpec((None, d, FFN_TF), lambda c, f, ce, cv: (ce[c], 0, f_idx(c, f, cv))),
            pl.BlockSpec((None, d, FFN_TF), lambda c, f, ce, cv: (ce[c], 0, f_idx(c, f, cv))),
            pl.BlockSpec((None, FFN_TF, d), lambda c, f, ce, cv: (ce[c], f_idx(c, f, cv), 0)),
        ],
        out_specs=pl.BlockSpec((FFN_ROWS, d), lambda c, f, ce, cv: (c, 0), pipeline_mode=pl.Buffered(1)),
    )
    return pl.pallas_call(
        _swiglu_kernel,
        out_shape=jax.ShapeDtypeStruct((rows, d), jnp.float32),
        grid_spec=grid_spec,
        compiler_params=_cparams("parallel", "arbitrary"),
    )(chunk_expert, chunk_valid, xs, wg, wu, wd)


def dense_swiglu(h, wg, wu, wd):
    m, d = h.shape
    n_chunks = pl.cdiv(m, FFN_ROWS)
    pad = n_chunks * FFN_ROWS - m
    xs = jnp.pad(h, ((0, pad), (0, 0)))
    starts = jnp.arange(n_chunks, dtype=jnp.int32) * FFN_ROWS
    valid = (jnp.clip(m - starts, 0, FFN_ROWS) + FFN_SUB - 1) // FFN_SUB
    out = swiglu_chunks(xs, wg[None], wu[None], wd[None], jnp.zeros((n_chunks,), jnp.int32),
                        valid.astype(jnp.int32))
    return out[:m]


def moe_swiglu_rows(h, logits, wg, wu, wd):
    m, d = h.shape
    top_v, top_i = lax.top_k(logits, TOP_K)
    top_w = jax.nn.softmax(top_v, axis=-1)
    n_pairs = m * TOP_K
    max_chunks = (n_pairs + N_EXPERTS * (FFN_ROWS - 1)) // FFN_ROWS
    flat_e = top_i.reshape(-1).astype(jnp.int32)
    order = jnp.argsort(flat_e, stable=True).astype(jnp.int32)
    counts = jnp.sum(jax.nn.one_hot(flat_e, N_EXPERTS, dtype=jnp.int32), axis=0)
    chunks_e = (counts + FFN_ROWS - 1) // FFN_ROWS
    chunk_end = jnp.cumsum(chunks_e)
    chunk_start = chunk_end - chunks_e
    pair_start = jnp.cumsum(counts) - counts
    sorted_e = flat_e[order]
    rank = jnp.arange(n_pairs, dtype=jnp.int32) - pair_start[sorted_e]
    dest_sorted = chunk_start[sorted_e] * FFN_ROWS + rank
    src_row = jnp.zeros((max_chunks * FFN_ROWS,), jnp.int32).at[dest_sorted].set(order // TOP_K)
    pos = jnp.zeros((n_pairs,), jnp.int32).at[order].set(dest_sorted).reshape(m, TOP_K)

    cidx = jnp.arange(max_chunks, dtype=jnp.int32)
    total = chunk_end[-1]
    ce = jnp.sum((cidx[:, None] >= chunk_end[None, :]).astype(jnp.int32), axis=1)
    last_e = jnp.sum((total - 1 >= chunk_end).astype(jnp.int32))
    ce = jnp.where(cidx < total, ce, last_e).astype(jnp.int32)
    rows_in = jnp.clip(counts[ce] - (cidx - chunk_start[ce]) * FFN_ROWS, 0, FFN_ROWS)
    cv = jnp.where(cidx < total, (rows_in + FFN_SUB - 1) // FFN_SUB, 0).astype(jnp.int32)

    xs = jnp.take(h, src_row, axis=0)
    ys = swiglu_chunks(xs, wg, wu, wd, ce, cv)
    y2 = jnp.take(ys, pos.reshape(-1), axis=0).reshape(m, TOP_K, d)
    return y2[:, 0] * top_w[:, 0:1] + y2[:, 1] * top_w[:, 1:2]


def _causal_conv(x, prev, w, b=None):
    L = x.shape[1]
    xp = jnp.concatenate([prev.astype(x.dtype), x], axis=1)
    out = xp[:, 0:L] * w[0]
    for j in range(1, CONV_W):
        out = out + xp[:, j:j + L] * w[j]
    if b is not None:
        out = out + b
    return out, xp[:, L:]


def _to_chunks(a, chunk):
    b, L = a.shape[:2]
    return jnp.moveaxis(a.reshape((b, L // chunk, chunk) + a.shape[2:]), 1, 0)


def _from_chunks(a):
    nc, b, q = a.shape[:3]
    return jnp.moveaxis(a, 0, 1).reshape((b, nc * q) + a.shape[3:])


def _segmented_scan(body, xs, carry, segments):
    outs = []
    for start, length, chunk in segments:
        seg = tuple(lax.slice_in_dim(a, start, start + length, axis=1) for a in xs)
        carry, ys = lax.scan(body, carry, tuple(_to_chunks(a, chunk) for a in seg))
        outs.append(_from_chunks(ys))
    y = outs[0] if len(outs) == 1 else jnp.concatenate(outs, axis=1)
    return y, carry


_HI = lax.Precision.HIGHEST


def _s5_mixer(u, h0_re, h0_im, segments, p):
    bsz, L, _ = u.shape
    lam_re = p['s5_lambda_re']
    lam_im = p['s5_lambda_im']
    step = jnp.exp(p['s5_log_step'])[:, None]
    mag = jnp.exp(lam_re * step)
    a_re = mag * jnp.cos(lam_im * step)
    a_im = mag * jnp.sin(lam_im * step)
    den = lam_re * lam_re + lam_im * lam_im
    f_re = ((a_re - 1.0) * lam_re + a_im * lam_im) / den
    f_im = (a_im * lam_re - (a_re - 1.0) * lam_im) / den
    b_re, b_im, c_re, c_im = p['s5_b_re'], p['s5_b_im'], p['s5_c_re'], p['s5_c_im']
    bb_re = f_re[..., None] * b_re - f_im[..., None] * b_im
    bb_im = f_re[..., None] * b_im + f_im[..., None] * b_re

    def combine(e1, e2):
        a1r, a1i, s1r, s1i = e1
        a2r, a2i, s2r, s2i = e2
        return (a2r * a1r - a2i * a1i, a2r * a1i + a2i * a1r,
                a2r * s1r - a2i * s1i + s2r, a2r * s1i + a2i * s1r + s2i)

    def body(carry, inp):
        h_re, h_im = carry
        (u_c,) = inp
        bu_re = jnp.einsum('gpj,bqgj->bqgp', bb_re, u_c, precision=_HI)
        bu_im = jnp.einsum('gpj,bqgj->bqgp', bb_im, u_c, precision=_HI)
        pw_re, pw_im, s_re, s_im = lax.associative_scan(
            combine,
            (jnp.broadcast_to(a_re, bu_re.shape), jnp.broadcast_to(a_im, bu_im.shape), bu_re, bu_im),
            axis=1)
        hr = s_re + pw_re * h_re[:, None] - pw_im * h_im[:, None]
        hi = s_im + pw_re * h_im[:, None] + pw_im * h_re[:, None]
        y = (jnp.einsum('gjp,bqgp->bqgj', c_re, hr, precision=_HI)
             - jnp.einsum('gjp,bqgp->bqgj', c_im, hi, precision=_HI))
        return (hr[:, -1], hi[:, -1]), y

    ug = u.reshape(bsz, L, S5_GROUPS, S5_GROUP)
    y, (h_re, h_im) = _segmented_scan(body, (ug,), (h0_re, h0_im), segments)
    return y.reshape(bsz, L, S5_DIM), h_re, h_im


def _ssd_mixer(xm, dt, bm, cm, h0, segments, a_log, d_skip):
    R = MB_HEADS // MB_GROUPS
    a_neg = -jnp.exp(a_log).reshape(MB_GROUPS, R)
    d = d_skip.reshape(MB_GROUPS, R)

    def body(h, inp):
        x, dtc, bc, cc = inp
        bsz, Q = x.shape[:2]
        x = x.reshape(bsz, Q, MB_GROUPS, R, MB_HEADDIM)
        dtc = dtc.reshape(bsz, Q, MB_GROUPS, R)
        cum = jnp.cumsum(dtc * a_neg, axis=1)
        idx = jnp.arange(Q)
        causal = (idx[:, None] >= idx[None, :])[None, :, :, None, None]
        seg = jnp.where(causal, cum[:, :, None] - cum[:, None, :], -jnp.inf)
        cb = jnp.einsum('bqgn,bkgn->bqkg', cc, bc, precision=_HI)
        m = cb[..., None] * jnp.exp(seg) * dtc[:, None]
        y = jnp.einsum('bqkgr,bkgrp->bqgrp', m, x, precision=_HI)
        y = y + jnp.einsum('bqgn,bgrpn->bqgrp', cc, h, precision=_HI) * jnp.exp(cum)[..., None]
        y = y + d[..., None] * x
        w_end = jnp.exp(cum[:, -1:] - cum) * dtc
        h_new = h * jnp.exp(cum[:, -1])[..., None, None] + jnp.einsum(
            'bkgn,bkgrp->bgrpn', bc, x * w_end[..., None], precision=_HI)
        return h_new, y.reshape(bsz, Q, MB_HEADS, MB_HEADDIM)

    bsz = xm.shape[0]
    h0g = h0.reshape(bsz, MB_GROUPS, R, MB_HEADDIM, MB_STATE)
    y, h = _segmented_scan(body, (xm, dt, bm, cm), h0g, segments)
    return y, h.reshape(bsz, MB_HEADS, MB_HEADDIM, MB_STATE)


def _gdn_mixer(q, k, v, g, beta, s0, segments):
    def body(S, inp):
        qc, kc, vc, gch, bch = (jnp.moveaxis(a, 2, 1) for a in inp)
        Q = qc.shape[2]
        gc = jnp.cumsum(gch, axis=-1)
        idx = jnp.arange(Q)
        causal = idx[:, None] >= idx[None, :]
        strict = idx[:, None] > idx[None, :]
        decay = jnp.exp(jnp.where(causal, gc[..., :, None] - gc[..., None, :], -jnp.inf))
        kb = kc * bch[..., None]
        lower = jnp.where(strict, jnp.einsum('bhik,bhjk->bhij', kb, kc, precision=_HI) * decay, 0.0)
        system = lower + jnp.eye(Q, dtype=jnp.float32)
        rhs = jnp.concatenate([vc * bch[..., None], kb * jnp.exp(gc)[..., None]], axis=-1)
        sol = lax.linalg.triangular_solve(system, rhs, left_side=True, lower=True, unit_diagonal=True)
        uu, ww = sol[..., :GDN_DV], sol[..., GDN_DV:]
        v_new = uu - jnp.einsum('bhik,bhkv->bhiv', ww, S, precision=_HI)
        attn = jnp.where(causal, jnp.einsum('bhik,bhjk->bhij', qc, kc, precision=_HI) * decay, 0.0)
        o = (jnp.einsum('bhik,bhkv->bhiv', qc * jnp.exp(gc)[..., None], S, precision=_HI)
             + jnp.einsum('bhij,bhjv->bhiv', attn, v_new, precision=_HI))
        g_last = gc[..., -1]
        S_new = S * jnp.exp(g_last)[..., None, None] + jnp.einsum(
            'bhjk,bhjv->bhkv', kc * jnp.exp(g_last[..., None] - gc)[..., None], v_new, precision=_HI)
        return S_new, jnp.moveaxis(o, 1, 2)

    return _segmented_scan(body, (q, k, v, g, beta), s0, segments)


def _rms_norm_f32(x, w):
    y = x * lax.rsqrt(jnp.mean(x * x, axis=-1, keepdims=True) + NORM_EPS)
    return y * w


def _s5_ssd_mix(proj, st, segments, p):
    s5_re0, s5_im0, ssd0, conv0 = st
    bsz, L, _ = proj.shape
    o1 = S5_DIM
    o2 = o1 + MB_DIM
    o3 = o2 + MB_CONV_DIM
    u = proj[..., :o1]
    z = proj[..., o1:o2]
    xbc, conv_new = _causal_conv(proj[..., o2:o3], conv0, p['mb_conv_w'], p['mb_conv_b'])
    dt_raw = proj[..., o3:]
    y_s5, s5_re, s5_im = _s5_mixer(u, s5_re0, s5_im0, segments, p)
    y_s5 = y_s5 + p['s5_d'] * u
    gl = jax.nn.gelu(y_s5)
    y_s5 = gl * jax.nn.sigmoid(jnp.dot(gl, p['s5_glu_w']) + p['s5_glu_b'])
    y_s5 = _rms_norm_f32(y_s5, p['s5_norm'])
    xbc = jax.nn.silu(xbc)
    gn = MB_GROUPS * MB_STATE
    xm = xbc[..., :MB_DIM].reshape(bsz, L, MB_HEADS, MB_HEADDIM)
    bm = xbc[..., MB_DIM:MB_DIM + gn].reshape(bsz, L, MB_GROUPS, MB_STATE)
    cm = xbc[..., MB_DIM + gn:].reshape(bsz, L, MB_GROUPS, MB_STATE)
    dt = jax.nn.softplus(dt_raw + p['mb_dt_bias'])
    y_mb, ssd = _ssd_mixer(xm, dt, bm, cm, ssd0, segments, p['mb_a_log'], p['mb_d'])
    y_mb = y_mb.reshape(bsz, L, MB_DIM) * jax.nn.silu(z)
    gsz = MB_DIM // MB_GROUPS
    y_mb = _rms_norm_f32(y_mb.reshape(bsz, L, MB_GROUPS, gsz),
                         p['mb_norm'].reshape(MB_GROUPS, gsz)).reshape(bsz, L, MB_DIM)
    return jnp.concatenate([y_s5, y_mb], axis=-1), (s5_re, s5_im, ssd, conv_new)


def _gdn_mix(proj, st, segments, p):
    s0, conv0 = st
    bsz, L, _ = proj.shape
    o1 = GDN_CONV_DIM
    o2 = o1 + GDN_V_DIM
    o3 = o2 + GDN_V_HEADS
    qkv, conv_new = _causal_conv(proj[..., :o1], conv0, p['gdn_conv_w'])
    qkv = jax.nn.silu(qkv)
    z = proj[..., o1:o2].reshape(bsz, L, GDN_V_HEADS, GDN_DV)
    b_raw = proj[..., o2:o3]
    a_raw = proj[..., o3:]

    def l2n(x):
        return x * lax.rsqrt(jnp.sum(x * x, axis=-1, keepdims=True) + 1e-6)

    q = l2n(qkv[..., :GDN_QK_DIM].reshape(bsz, L, GDN_QK_HEADS, GDN_DK)) * (GDN_DK ** -0.5)
    k = l2n(qkv[..., GDN_QK_DIM:2 * GDN_QK_DIM].reshape(bsz, L, GDN_QK_HEADS, GDN_DK))
    v = qkv[..., 2 * GDN_QK_DIM:].reshape(bsz, L, GDN_V_HEADS, GDN_DV)
    rep = GDN_V_HEADS // GDN_QK_HEADS
    q = jnp.repeat(q, rep, axis=2)
    k = jnp.repeat(k, rep, axis=2)
    beta = jax.nn.sigmoid(b_raw)
    g = -jnp.exp(p['gdn_a_log']) * jax.nn.softplus(a_raw + p['gdn_dt_bias'])
    o, S = _gdn_mixer(q, k, v, g, beta, s0, segments)
    o = _rms_norm_f32(o, p['gdn_norm']) * jax.nn.silu(z)
    return o.reshape(bsz, L, GDN_V_DIM), (S, conv_new)


def _run_groups(mix_fn, proj, states, p, n_prompt, seq, n_dec, dec_len):
    n_main = n_prompt * seq
    n_meta = n_prompt * N_META
    main = proj[:n_main].reshape(n_prompt, seq, -1)
    meta = proj[n_main:n_main + n_meta].reshape(n_prompt, N_META, -1)
    samp = proj[n_main + n_meta:].reshape(n_dec, dec_len, -1)
    zero = tuple(jnp.zeros((n_prompt,) + s.shape[1:], s.dtype) for s in states)
    yp, stp = mix_fn(jnp.concatenate([meta, main], axis=1), zero, ((0, N_META, N_META), (N_META, seq, CHUNK)), p)
    ys, sts = mix_fn(samp, states, ((0, dec_len, dec_len),), p)
    d = yp.shape[-1]
    y = jnp.concatenate([yp[:, N_META:].reshape(n_main, d), yp[:, :N_META].reshape(n_meta, d),
                         ys.reshape(n_dec * dec_len, d)], axis=0)
    return y, stp, sts


def kernel(x_prompt, x_sample, state_s5_re, state_s5_im, state_ssd, cache_ssd_conv, state_gdn, cache_gdn_conv, meta_tokens, norm_mix0, w_in0, s5_lambda_re, s5_lambda_im, s5_log_step, s5_b_re, s5_b_im, s5_c_re, s5_c_im, s5_d, s5_glu_w, s5_glu_b, s5_norm, mb_conv_w, mb_conv_b, mb_dt_bias, mb_a_log, mb_d, mb_norm, w_out0, norm_ffn0, ffn_w_gate, ffn_w_up, ffn_w_down, norm_mix1, w_in1, gdn_conv_w, gdn_a_log, gdn_dt_bias, gdn_norm, w_out1, norm_ffn1, moe_router, moe_w_gate, moe_w_up, moe_w_down, norm_final):
    p = dict(s5_lambda_re=s5_lambda_re, s5_lambda_im=s5_lambda_im, s5_log_step=s5_log_step, s5_b_re=s5_b_re,
             s5_b_im=s5_b_im, s5_c_re=s5_c_re, s5_c_im=s5_c_im, s5_d=s5_d, s5_glu_w=s5_glu_w, s5_glu_b=s5_glu_b,
             s5_norm=s5_norm, mb_conv_w=mb_conv_w, mb_conv_b=mb_conv_b, mb_dt_bias=mb_dt_bias, mb_a_log=mb_a_log,
             mb_d=mb_d, mb_norm=mb_norm, gdn_conv_w=gdn_conv_w, gdn_a_log=gdn_a_log, gdn_dt_bias=gdn_dt_bias,
             gdn_norm=gdn_norm)
    n_prompt, seq, d = x_prompt.shape
    n_dec, dec_len, _ = x_sample.shape
    n_main = n_prompt * seq
    n_meta = n_prompt * N_META
    groups = (n_prompt, seq, n_dec, dec_len)

    x0 = jnp.concatenate([x_prompt.reshape(n_main, d), jnp.tile(meta_tokens, (n_prompt, 1)),
                          x_sample.reshape(n_dec * dec_len, d)], axis=0)

    h = rms_norm_rows(x0, norm_mix0, jnp.bfloat16)
    n_main_cols = S5_DIM + MB_DIM + MB_CONV_DIM
    proj0 = jnp.concatenate([matmul_rows(h, w_in0, ncols=n_main_cols),
                             matmul_rows(h, w_in0[:, n_main_cols:])], axis=1)
    mixed, (p_s5_re, p_s5_im, p_ssd, p_ssd_conv), (s_s5_re, s_s5_im, s_ssd, s_ssd_conv) = _run_groups(
        _s5_ssd_mix, proj0, (state_s5_re, state_s5_im, state_ssd, cache_ssd_conv), p, *groups)
    x1 = matmul_rows(mixed.astype(jnp.bfloat16), w_out0, tn=256, residual=x0)
    h = rms_norm_rows(x1, norm_ffn0, jnp.bfloat16)
    ffn = dense_swiglu(h, ffn_w_gate, ffn_w_up, ffn_w_down)

    x2, h = add_rms_norm_rows(x1, ffn, norm_mix1)
    n_main_cols = GDN_CONV_DIM + GDN_V_DIM
    proj1 = jnp.concatenate([matmul_rows(h, w_in1, ncols=n_main_cols),
                             matmul_rows(h, w_in1[:, n_main_cols:])], axis=1)
    mixed, (p_gdn, p_gdn_conv), (s_gdn, s_gdn_conv) = _run_groups(
        _gdn_mix, proj1, (state_gdn, cache_gdn_conv), p, *groups)
    x3 = matmul_rows(mixed.astype(jnp.bfloat16), w_out1, tn=256, residual=x2)
    h, logits = rms_norm_router_rows(x3, norm_ffn1, moe_router)
    moe = moe_swiglu_rows(h, logits, moe_w_gate, moe_w_up, moe_w_down)
    y = add_rms_norm_only_rows(x3, moe, norm_final, jnp.float32)

    y_prompt = y[:n_main].reshape(n_prompt, seq, d)
    y_sample = y[n_main + n_meta:].reshape(n_dec, dec_len, d)
    return (y_prompt, y_sample, p_s5_re, p_s5_im, p_ssd, p_ssd_conv, p_gdn, p_gdn_conv,
            s_s5_re, s_s5_im, s_ssd, s_ssd_conv, s_gdn, s_gdn_conv)
```

```python
import functools
import math

import jax
import jax.numpy as jnp
from jax import lax
from jax.experimental import pallas as pl
from jax.experimental.pallas import tpu as pltpu

D_MODEL = 4096
CHUNK = 64
N_META = 16
NORM_EPS = 1e-6
CONV_W = 4

S5_GROUP = 16
S5_DIM = D_MODEL // 2
S5_GROUPS = S5_DIM // S5_GROUP
S5_STATE = 64
MB_HEADDIM = 64
MB_DIM = D_MODEL
MB_HEADS = MB_DIM // MB_HEADDIM
MB_GROUPS = 8
MB_STATE = 128
MB_CONV_DIM = MB_DIM + 2 * MB_GROUPS * MB_STATE

GDN_QK_HEADS = 32
GDN_V_HEADS = 64
GDN_DK = 128
GDN_DV = 128
GDN_QK_DIM = GDN_QK_HEADS * GDN_DK
GDN_V_DIM = GDN_V_HEADS * GDN_DV
GDN_CONV_DIM = 2 * GDN_QK_DIM + GDN_V_DIM

N_EXPERTS = 8
TOP_K = 2

V7X_VMEM_LIMIT_BYTES = 56 * 1024 * 1024
MXU_COLS = 256
LANES = 128
SUBLANES = 8
ROW_TILE = 224
MM_ROWS = 1216
FFN_ROWS = 1024
FFN_SUB = 256
FFN_TF = MXU_COLS
S5_SUB = 16
S5_GB = MXU_COLS // S5_GROUP
S5_ROWS = 272

_HI = lax.Precision.HIGHEST


def _cparams(*sem):
    return pltpu.CompilerParams(dimension_semantics=sem, vmem_limit_bytes=V7X_VMEM_LIMIT_BYTES)


def _dot(a, b):
    return jnp.dot(a, b, precision=_HI, preferred_element_type=jnp.float32)


def _dot_nt(a, b):
    return lax.dot_general(a, b, (((1,), (1,)), ((), ())), precision=_HI, preferred_element_type=jnp.float32)


def _dot_tn(a, b):
    return lax.dot_general(a, b, (((0,), (0,)), ((), ())), precision=_HI, preferred_element_type=jnp.float32)


def _silu(x):
    return x * jax.nn.sigmoid(x)


def _softplus(x):
    return jnp.maximum(x, 0.0) + jnp.log1p(jnp.exp(-jnp.abs(x)))


def _rms(x, w):
    ms = jnp.mean(x * x, axis=-1, keepdims=True)
    return (x * lax.rsqrt(ms + NORM_EPS)) * w


def _norm_kernel(x_ref, w_ref, h_ref):
    h_ref[...] = _rms(x_ref[...], w_ref[...]).astype(h_ref.dtype)


def _add_norm_kernel(x_ref, d_ref, w_ref, xo_ref, h_ref):
    x = x_ref[...] + d_ref[...]
    xo_ref[...] = x
    h_ref[...] = _rms(x, w_ref[...]).astype(h_ref.dtype)


def _add_norm_only_kernel(x_ref, d_ref, w_ref, h_ref):
    h_ref[...] = _rms(x_ref[...] + d_ref[...], w_ref[...]).astype(h_ref.dtype)


def _norm_router_kernel(x_ref, w_ref, wr_ref, h_ref, lg_ref):
    h = _rms(x_ref[...], w_ref[...])
    h_ref[...] = h.astype(h_ref.dtype)
    lg_ref[...] = _dot(h, wr_ref[...])


def _row_spec(cols):
    return pl.BlockSpec((ROW_TILE, cols), lambda i: (i, 0))


def _full_spec(shape):
    return pl.BlockSpec(shape, lambda *_: (0,) * len(shape))


def rms_norm_rows(x, w, out_dtype):
    m, d = x.shape
    return pl.pallas_call(
        _norm_kernel,
        out_shape=jax.ShapeDtypeStruct((m, d), out_dtype),
        grid=(m // ROW_TILE,),
        in_specs=[_row_spec(d), _full_spec((1, d))],
        out_specs=_row_spec(d),
        compiler_params=_cparams("parallel"),
    )(x, w.reshape(1, d))


def add_rms_norm_rows(x, delta, w):
    m, d = x.shape
    return pl.pallas_call(
        _add_norm_kernel,
        out_shape=(jax.ShapeDtypeStruct((m, d), jnp.float32), jax.ShapeDtypeStruct((m, d), jnp.bfloat16)),
        grid=(m // ROW_TILE,),
        in_specs=[_row_spec(d), _row_spec(d), _full_spec((1, d))],
        out_specs=(_row_spec(d), _row_spec(d)),
        compiler_params=_cparams("parallel"),
    )(x, delta, w.reshape(1, d))


def add_rms_norm_only_rows(x, delta, w, out_dtype):
    m, d = x.shape
    return pl.pallas_call(
        _add_norm_only_kernel,
        out_shape=jax.ShapeDtypeStruct((m, d), out_dtype),
        grid=(m // ROW_TILE,),
        in_specs=[_row_spec(d), _row_spec(d), _full_spec((1, d))],
        out_specs=_row_spec(d),
        compiler_params=_cparams("parallel"),
    )(x, delta, w.reshape(1, d))


def rms_norm_router_rows(x, w, w_router):
    m, d = x.shape
    e = w_router.shape[1]
    return pl.pallas_call(
        _norm_router_kernel,
        out_shape=(jax.ShapeDtypeStruct((m, d), jnp.bfloat16), jax.ShapeDtypeStruct((m, e), jnp.float32)),
        grid=(m // ROW_TILE,),
        in_specs=[_row_spec(d), _full_spec((1, d)), _full_spec((d, e))],
        out_specs=(_row_spec(d), _row_spec(e)),
        compiler_params=_cparams("parallel"),
    )(x, w.reshape(1, d), w_router)


def _mm_kernel(x_ref, w_ref, o_ref):
    o_ref[...] = jnp.dot(x_ref[...], w_ref[...].astype(jnp.bfloat16),
                         preferred_element_type=jnp.float32).astype(o_ref.dtype)


def _mm_res_kernel(x_ref, w_ref, r_ref, o_ref):
    o_ref[...] = r_ref[...] + jnp.dot(x_ref[...], w_ref[...].astype(jnp.bfloat16),
                                      preferred_element_type=jnp.float32)


def matmul_rows(x, w, *, col0=0, ncols=None, tn=512, residual=None, out_dtype=jnp.float32):
    m, k = x.shape
    ncols = w.shape[1] - col0 if ncols is None else ncols
    if ncols < tn:
        tn = ncols
    assert m % MM_ROWS == 0 and ncols % tn == 0 and col0 % tn == 0
    cb = col0 // tn
    x_spec = pl.BlockSpec((MM_ROWS, k), lambda i, j: (i, 0), pipeline_mode=pl.Buffered(1))
    w_spec = pl.BlockSpec((k, tn), lambda i, j: (0, j + cb))
    o_spec = pl.BlockSpec((MM_ROWS, tn), lambda i, j: (i, j))
    grid = (m // MM_ROWS, ncols // tn)
    if residual is None:
        return pl.pallas_call(
            _mm_kernel, out_shape=jax.ShapeDtypeStruct((m, ncols), out_dtype), grid=grid,
            in_specs=[x_spec, w_spec], out_specs=o_spec,
            compiler_params=_cparams("parallel", "arbitrary"))(x, w)
    return pl.pallas_call(
        _mm_res_kernel, out_shape=jax.ShapeDtypeStruct((m, ncols), jnp.float32), grid=grid,
        in_specs=[x_spec, w_spec, o_spec], out_specs=o_spec,
        compiler_params=_cparams("parallel", "arbitrary"))(x, w, residual)


def _swiglu_kernel(ce_ref, cv_ref, x_ref, wg_ref, wu_ref, wd_ref, o_ref):
    c = pl.program_id(0)
    f = pl.program_id(1)
    n_valid = cv_ref[c]

    @pl.when(f == 0)
    def _():
        o_ref[...] = jnp.zeros_like(o_ref)

    @pl.when(n_valid > 0)
    def _():
        wg = wg_ref[...].astype(jnp.bfloat16)
        wu = wu_ref[...].astype(jnp.bfloat16)
        wd = wd_ref[...].astype(jnp.bfloat16)
        for s in range(FFN_ROWS // FFN_SUB):
            @pl.when(s < n_valid)
            def _():
                rows = pl.ds(s * FFN_SUB, FFN_SUB)
                xs = x_ref[rows, :]
                g = jnp.dot(xs, wg, preferred_element_type=jnp.float32)
                u = jnp.dot(xs, wu, preferred_element_type=jnp.float32)
                h = (_silu(g) * u).astype(jnp.bfloat16)
                o_ref[rows, :] += jnp.dot(h, wd, preferred_element_type=jnp.float32)


def swiglu_chunks(xs, wg, wu, wd, chunk_expert, chunk_valid):
    rows, d = xs.shape
    n_chunks = rows // FFN_ROWS
    ff = wg.shape[2]
    nf = ff // FFN_TF

    def f_idx(c, f, cv):
        return jnp.where(cv[c] > 0, f, nf - 1)

    grid_spec = pltpu.PrefetchScalarGridSpec(
        num_scalar_prefetch=2,
        grid=(n_chunks, nf),
        in_specs=[
            pl.BlockSpec((FFN_ROWS, d), lambda c, f, ce, cv: (c, 0), pipeline_mode=pl.Buffered(1)),
            pl.BlockSpec((None, d, FFN_TF), lambda c, f, ce, cv: (ce[c], 0, f_idx(c, f, cv))),
            pl.BlockSpec((None, d, FFN_TF), lambda c, f, ce, cv: (ce[c], 0, f_idx(c, f, cv))),
            pl.BlockSpec((None, FFN_TF, d), lambda c, f, ce, cv: (ce[c], f_idx(c, f, cv), 0)),
        ],
        out_specs=pl.BlockSpec((FFN_ROWS, d), lambda c, f, ce, cv: (c, 0), pipeline_mode=pl.Buffered(1)),
    )
    return pl.pallas_call(
        _swiglu_kernel,
        out_shape=jax.ShapeDtypeStruct((rows, d), jnp.float32),
        grid_spec=grid_spec,
        compiler_params=_cparams("parallel", "arbitrary"),
    )(chunk_expert, chunk_valid, xs, wg, wu, wd)


def dense_swiglu(h, wg, wu, wd):
    m, d = h.shape
    n_chunks = pl.cdiv(m, FFN_ROWS)
    pad = n_chunks * FFN_ROWS - m
    xs = jnp.pad(h, ((0, pad), (0, 0)))
    starts = jnp.arange(n_chunks, dtype=jnp.int32) * FFN_ROWS
    valid = (jnp.clip(m - starts, 0, FFN_ROWS) + FFN_SUB - 1) // FFN_SUB
    out = swiglu_chunks(xs, wg[None], wu[None], wd[None], jnp.zeros((n_chunks,), jnp.int32),
                        valid.astype(jnp.int32))
    return out[:m]


def moe_swiglu_rows(h, logits, wg, wu, wd):
    m, d = h.shape
    top_v, top_i = lax.top_k(logits, TOP_K)
    top_w = jax.nn.softmax(top_v, axis=-1)
    n_pairs = m * TOP_K
    max_chunks = (n_pairs + N_EXPERTS * (FFN_ROWS - 1)) // FFN_ROWS
    flat_e = top_i.reshape(-1).astype(jnp.int32)
    order = jnp.argsort(flat_e, stable=True).astype(jnp.int32)
    counts = jnp.sum(jax.nn.one_hot(flat_e, N_EXPERTS, dtype=jnp.int32), axis=0)
    chunks_e = (counts + FFN_ROWS - 1) // FFN_ROWS
    chunk_end = jnp.cumsum(chunks_e)
    chunk_start = chunk_end - chunks_e
    pair_start = jnp.cumsum(counts) - counts
    sorted_e = flat_e[order]
    rank = jnp.arange(n_pairs, dtype=jnp.int32) - pair_start[sorted_e]
    dest_sorted = chunk_start[sorted_e] * FFN_ROWS + rank
    src_row = jnp.zeros((max_chunks * FFN_ROWS,), jnp.int32).at[dest_sorted].set(order // TOP_K)
    pos = jnp.zeros((n_pairs,), jnp.int32).at[order].set(dest_sorted).reshape(m, TOP_K)

    cidx = jnp.arange(max_chunks, dtype=jnp.int32)
    total = chunk_end[-1]
    ce = jnp.sum((cidx[:, None] >= chunk_end[None, :]).astype(jnp.int32), axis=1)
    last_e = jnp.sum((total - 1 >= chunk_end).astype(jnp.int32))
    ce = jnp.where(cidx < total, ce, last_e).astype(jnp.int32)
    rows_in = jnp.clip(counts[ce] - (cidx - chunk_start[ce]) * FFN_ROWS, 0, FFN_ROWS)
    cv = jnp.where(cidx < total, (rows_in + FFN_SUB - 1) // FFN_SUB, 0).astype(jnp.int32)

    xs = jnp.take(h, src_row, axis=0)
    ys = swiglu_chunks(xs, wg, wu, wd, ce, cv)
    y2 = jnp.take(ys, pos.reshape(-1), axis=0).reshape(m, TOP_K, d)
    return y2[:, 0] * top_w[:, 0:1] + y2[:, 1] * top_w[:, 1:2]


def _conv_rows_into(xp_ref, cur_ref, halo_ref, prev_ref, first_chunk, q_len):
    @pl.when(first_chunk)
    def _():
        xp_ref[0:SUBLANES, :] = prev_ref[0]

    @pl.when(jnp.logical_not(first_chunk))
    def _():
        xp_ref[0:SUBLANES, :] = halo_ref[...]

    xp_ref[SUBLANES:SUBLANES + q_len, :] = cur_ref[...]


def _conv_tile(xp_ref, w_ref, lanes, q_len):
    base = SUBLANES - (CONV_W - 1)
    acc = xp_ref[base:base + q_len, lanes] * w_ref[0:1, lanes]
    for j in range(1, CONV_W):
        acc = acc + xp_ref[base + j:base + j + q_len, lanes] * w_ref[j:j + 1, lanes]
    return acc


def _chunk_specs(row_block0, n_chunk, q_len, cols):
    per8 = q_len // SUBLANES

    def cur(s, c):
        return (row_block0 + s * n_chunk + c, 0)

    def halo(s, c):
        return (jnp.maximum((row_block0 + s * n_chunk + c) * per8 - 1, 0), 0)

    return (pl.BlockSpec((q_len, cols), cur), pl.BlockSpec((SUBLANES, cols), halo),
            pl.BlockSpec((1, SUBLANES, cols), lambda s, c: (s, 0, 0)))


def _row_chunk_spec(row_block0, n_chunk, q_len, cols):
    return pl.BlockSpec((q_len, cols), lambda s, c: (row_block0 + s * n_chunk + c, 0))


def _per_chunk_spec(n_chunk, shape):
    nd = len(shape)
    return pl.BlockSpec((1,) + tuple(shape), lambda s, c: (s * n_chunk + c,) + (0,) * nd)


def _per_seq_spec(shape):
    nd = len(shape)
    return pl.BlockSpec((1,) + tuple(shape), lambda s, c: (s,) + (0,) * nd)


def _iota2(shape):
    return lax.broadcasted_iota(jnp.int32, shape, 0), lax.broadcasted_iota(jnp.int32, shape, 1)


def _gdn_prep_kernel(cur_ref, halo_ref, prev_ref, w_ref, braw_ref, araw_ref, alog_ref, dtb_ref,
                     qh_ref, kh_ref, vh_ref, gt_ref, bt_ref, xp_ref, *, q_len, n_qk, n_v):
    _conv_rows_into(xp_ref, cur_ref, halo_ref, prev_ref, pl.program_id(1) == 0, q_len)

    def tile(t):
        return _silu(_conv_tile(xp_ref, w_ref, slice(t * LANES, (t + 1) * LANES), q_len))

    def l2n(x):
        return x * lax.rsqrt(jnp.sum(x * x, axis=-1, keepdims=True) + 1e-6)

    for h in range(n_qk):
        qh_ref[0, h] = l2n(tile(h)) * (GDN_DK ** -0.5)
    for h in range(n_qk):
        kh_ref[0, h] = l2n(tile(n_qk + h))
    for h in range(n_v):
        vh_ref[0, h] = tile(2 * n_qk + h)

    g = -jnp.exp(alog_ref[...]) * _softplus(araw_ref[...] + dtb_ref[...])
    beta = jax.nn.sigmoid(braw_ref[...])
    ri, ci = _iota2((q_len, q_len))
    gt_ref[0] = _dot_tn(g, (ri <= ci).astype(jnp.float32))
    bt_ref[0] = _dot_tn(beta, (ri == ci).astype(jnp.float32))


def _unit_lower_inverse(a, q_len, ri, ci, eye_f):
    blk = 16
    if q_len <= blk:
        a_d, a_o = a, None
    else:
        same = (ri // blk) == (ci // blk)
        a_d = jnp.where(same, a, 0.0)
        a_o = a - a_d
    p = eye_f + a_d
    x = a_d
    for _ in range(int(math.log2(min(q_len, blk))) - 1):
        x = _dot(x, x)
        p = p + _dot(p, x)
    if a_o is None:
        return p
    y = _dot(p, a_o)
    m = eye_f + y
    for _ in range(int(math.log2(q_len // blk)) - 1):
        y = _dot(y, y)
        m = m + _dot(m, y)
    return _dot(m, p)


def _gdn_chunk_kernel(qh_ref, kh_ref, vh_ref, gt_ref, bt_ref, z_ref, nw_ref, s0_ref, o_ref, s_ref, o_scr,
                      *, q_len, n_qk, rep):
    @pl.when(pl.program_id(1) == 0)
    def _():
        s_ref[...] = s0_ref[...]

    ri, ci = _iota2((q_len, q_len))
    causal = ri >= ci
    strict = ri > ci
    eye = ri == ci
    eye_f = eye.astype(jnp.float32)

    def qk_head(hq, carry):
        q = qh_ref[0, hq]
        k = kh_ref[0, hq]
        kk = _dot_nt(k, k)
        qk = _dot_nt(q, k)
        for r in range(rep):
            h = hq * rep + r
            g_row = gt_ref[0, pl.ds(h, 1), :]
            b_row = bt_ref[0, pl.ds(h, 1), :]
            g_col = jnp.sum(jnp.where(eye, g_row, 0.0), axis=1, keepdims=True)
            b_col = jnp.sum(jnp.where(eye, b_row, 0.0), axis=1, keepdims=True)
            decay = jnp.where(causal, jnp.exp(jnp.where(causal, g_col - g_row, 0.0)), 0.0)
            a = jnp.where(strict, -(b_col * kk) * decay, 0.0)
            t_inv = _unit_lower_inverse(a, q_len, ri, ci, eye_f)
            v = vh_ref[0, h]
            e_col = jnp.exp(g_col)
            uu = _dot(t_inv, v * b_col)
            ww = _dot(t_inv, k * (b_col * e_col))
            s = s_ref[0, h]
            v_new = uu - _dot(ww, s)
            attn = jnp.where(causal, qk * decay, 0.0)
            o = _dot(q * e_col, s) + _dot(attn, v_new)
            g_last = g_row[:, q_len - 1:q_len]
            s_ref[0, h] = s * jnp.exp(g_last) + _dot_tn(k * jnp.exp(g_last - g_col), v_new)
            o_scr[h] = _rms(o, nw_ref[...])
        return carry

    lax.fori_loop(0, n_qk, qk_head, 0)
    for h in range(n_qk * rep):
        lanes = slice(h * LANES, (h + 1) * LANES)
        o_ref[:, lanes] = (o_scr[h] * _silu(z_ref[:, lanes])).astype(o_ref.dtype)


def gdn_group(qkv, z, b_raw, a_raw, prev8, s0, p, *, row0, n_seq, n_chunk, q_len):
    n_qk, n_v = GDN_QK_HEADS, GDN_V_HEADS
    cdim = qkv.shape[1]
    nc = n_seq * n_chunk
    rb0 = row0 // q_len
    grid = (n_seq, n_chunk)
    cur_spec, halo_spec, prev_spec = _chunk_specs(rb0, n_chunk, q_len, cdim)
    small = _row_chunk_spec(rb0, n_chunk, q_len, n_v)
    f32 = jnp.float32
    qh, kh, vh, gt, bt = pl.pallas_call(
        functools.partial(_gdn_prep_kernel, q_len=q_len, n_qk=n_qk, n_v=n_v),
        out_shape=(jax.ShapeDtypeStruct((nc, n_qk, q_len, GDN_DK), f32),
                   jax.ShapeDtypeStruct((nc, n_qk, q_len, GDN_DK), f32),
                   jax.ShapeDtypeStruct((nc, n_v, q_len, GDN_DV), f32),
                   jax.ShapeDtypeStruct((nc, n_v, q_len), f32),
                   jax.ShapeDtypeStruct((nc, n_v, q_len), f32)),
        grid=grid,
        in_specs=[cur_spec, halo_spec, prev_spec, _full_spec((CONV_W, cdim)), small, small,
                  _full_spec((1, n_v)), _full_spec((1, n_v))],
        out_specs=(_per_chunk_spec(n_chunk, (n_qk, q_len, GDN_DK)), _per_chunk_spec(n_chunk, (n_qk, q_len, GDN_DK)),
                   _per_chunk_spec(n_chunk, (n_v, q_len, GDN_DV)), _per_chunk_spec(n_chunk, (n_v, q_len)),
                   _per_chunk_spec(n_chunk, (n_v, q_len))),
        scratch_shapes=[pltpu.VMEM((SUBLANES + q_len, cdim), f32)],
        compiler_params=_cparams("parallel", "arbitrary"),
    )(qkv, qkv, prev8, p['gdn_conv_w'], b_raw, a_raw, p['gdn_a_log'].reshape(1, n_v),
      p['gdn_dt_bias'].reshape(1, n_v))

    vdim = n_v * GDN_DV
    o, s_fin = pl.pallas_call(
        functools.partial(_gdn_chunk_kernel, q_len=q_len, n_qk=n_qk, rep=n_v // n_qk),
        out_shape=(jax.ShapeDtypeStruct((nc * q_len, vdim), jnp.bfloat16),
                   jax.ShapeDtypeStruct((n_seq, n_v, GDN_DK, GDN_DV), f32)),
        grid=grid,
        in_specs=[_per_chunk_spec(n_chunk, (n_qk, q_len, GDN_DK)), _per_chunk_spec(n_chunk, (n_qk, q_len, GDN_DK)),
                  _per_chunk_spec(n_chunk, (n_v, q_len, GDN_DV)), _per_chunk_spec(n_chunk, (n_v, q_len)),
                  _per_chunk_spec(n_chunk, (n_v, q_len)), _row_chunk_spec(rb0, n_chunk, q_len, vdim),
                  _full_spec((1, GDN_DV)), _per_seq_spec((n_v, GDN_DK, GDN_DV))],
        out_specs=(_row_chunk_spec(0, n_chunk, q_len, vdim), _per_seq_spec((n_v, GDN_DK, GDN_DV))),
        scratch_shapes=[pltpu.VMEM((n_v, q_len, GDN_DV), f32)],
        compiler_params=_cparams("parallel", "arbitrary"),
    )(qh, kh, vh, gt, bt, z, p['gdn_norm'].reshape(1, GDN_DV), s0)
    return o, s_fin


def _ssd_prep_kernel(cur_ref, halo_ref, prev_ref, w_ref, b_ref, dtraw_ref, dtb_ref, alog_ref,
                     xh_ref, bh_ref, ch_ref, cum_ref, dt_ref, cumt_ref, dtt_ref, xp_ref,
                     *, q_len, n_pair, n_grp):
    _conv_rows_into(xp_ref, cur_ref, halo_ref, prev_ref, pl.program_id(1) == 0, q_len)

    def tile(t):
        lanes = slice(t * LANES, (t + 1) * LANES)
        return _silu(_conv_tile(xp_ref, w_ref, lanes, q_len) + b_ref[0:1, lanes])

    for t in range(n_pair):
        xh_ref[0, t] = tile(t)
    for t in range(n_grp):
        bh_ref[0, t] = tile(n_pair + t)
    for t in range(n_grp):
        ch_ref[0, t] = tile(n_pair + n_grp + t)

    dt = _softplus(dtraw_ref[...] + dtb_ref[...])
    dta = dt * (-jnp.exp(alog_ref[...]))
    ri, ci = _iota2((q_len, q_len))
    cum_ref[0] = _dot((ri >= ci).astype(jnp.float32), dta)
    dt_ref[0] = dt
    r2, c2 = _iota2((q_len, 2 * q_len))
    c2 = jnp.where(c2 >= q_len, c2 - q_len, c2)
    cumt_ref[0] = _dot_tn(dta, (r2 <= c2).astype(jnp.float32))
    dtt_ref[0] = _dot_tn(dt, (r2 == c2).astype(jnp.float32))


def _ssd_chunk_kernel(xh_ref, bh_ref, ch_ref, cum_ref, dt_ref, cumt_ref, dtt_ref, z_ref, d_ref, nw_ref, s0_ref,
                      o_ref, s_ref, y_scr, *, q_len, n_pair, n_grp):
    @pl.when(pl.program_id(1) == 0)
    def _():
        s_ref[...] = s0_ref[...]

    n_head = 2 * n_pair
    pairs_per_grp = n_pair // n_grp
    half = LANES // 2
    r2, c2 = _iota2((q_len, 2 * q_len))
    second2 = c2 >= q_len
    causal2 = r2 >= jnp.where(second2, c2 - q_len, c2)
    lane_row2 = lax.broadcasted_iota(jnp.int32, (1, 2 * q_len), 1) >= q_len
    hx, lx = _iota2((n_head, LANES))
    h2, l2 = _iota2((n_head, 2 * q_len))
    rx, cx = _iota2((LANES, LANES))
    eye_x = rx == cx
    lane_x = lax.broadcasted_iota(jnp.int32, (1, LANES), 1)
    cum = cum_ref[0]
    dt = dt_ref[0]

    def pair(p, carry):
        g = p // pairs_per_grp
        h0 = 2 * p
        x = xh_ref[0, p]
        bg = bh_ref[0, g]
        cg = ch_ref[0, g]
        cb2 = _dot_nt(cg, jnp.concatenate([bg, bg], axis=0))
        crow2 = jnp.where(lane_row2, cumt_ref[0, pl.ds(h0 + 1, 1), :], cumt_ref[0, pl.ds(h0, 1), :])
        dtrow2 = jnp.where(lane_row2, dtt_ref[0, pl.ds(h0 + 1, 1), :], dtt_ref[0, pl.ds(h0, 1), :])
        sel2 = (h2 == h0 + (l2 >= q_len).astype(jnp.int32)).astype(jnp.float32)
        selx = (hx == h0 + (lx >= half).astype(jnp.int32)).astype(jnp.float32)
        ccol2 = _dot(cum, sel2)
        ccolx = _dot(cum, selx)
        dtcolx = _dot(dt, selx)
        m = jnp.where(causal2, cb2 * jnp.exp(jnp.where(causal2, ccol2 - crow2, 0.0)) * dtrow2, 0.0)
        xstack = jnp.concatenate([jnp.where(lane_x < half, x, 0.0), jnp.where(lane_x >= half, x, 0.0)], axis=0)
        s = s_ref[0, p]
        y = _dot(m, xstack) + _dot_nt(cg, s) * jnp.exp(ccolx) + d_ref[0, pl.ds(p, 1), :] * x
        clastx = ccolx[q_len - 1:q_len, :]
        wx = jnp.exp(clastx - ccolx) * dtcolx
        dec_col = jnp.sum(jnp.where(eye_x, jnp.exp(clastx), 0.0), axis=1, keepdims=True)
        s_ref[0, p] = s * dec_col + _dot_tn(x * wx, bg)
        y_scr[p] = y
        return carry

    lax.fori_loop(0, n_pair, pair, 0)

    gsz = pairs_per_grp * LANES
    for g in range(n_grp):
        parts = []
        ss = None
        for i in range(pairs_per_grp):
            t = g * pairs_per_grp + i
            lanes = slice(t * LANES, (t + 1) * LANES)
            yz = y_scr[t] * _silu(z_ref[:, lanes])
            parts.append((lanes, yz))
            sq = jnp.sum(yz * yz, axis=-1, keepdims=True)
            ss = sq if ss is None else ss + sq
        scale = lax.rsqrt(ss / gsz + NORM_EPS)
        for lanes, yz in parts:
            o_ref[:, lanes] = ((yz * scale) * nw_ref[0:1, lanes]).astype(o_ref.dtype)


def ssd_group(xbc, z, dt_raw, prev8, s0, p, *, row0, n_seq, n_chunk, q_len):
    n_pair = MB_HEADS // 2
    n_grp = MB_GROUPS
    n_head = MB_HEADS
    cdim = xbc.shape[1]
    nc = n_seq * n_chunk
    rb0 = row0 // q_len
    grid = (n_seq, n_chunk)
    f32 = jnp.float32
    cur_spec, halo_spec, prev_spec = _chunk_specs(rb0, n_chunk, q_len, cdim)
    xh, bh, ch, cum, dt, cumt, dtt = pl.pallas_call(
        functools.partial(_ssd_prep_kernel, q_len=q_len, n_pair=n_pair, n_grp=n_grp),
        out_shape=(jax.ShapeDtypeStruct((nc, n_pair, q_len, LANES), f32),
                   jax.ShapeDtypeStruct((nc, n_grp, q_len, MB_STATE), f32),
                   jax.ShapeDtypeStruct((nc, n_grp, q_len, MB_STATE), f32),
                   jax.ShapeDtypeStruct((nc, q_len, n_head), f32),
                   jax.ShapeDtypeStruct((nc, q_len, n_head), f32),
                   jax.ShapeDtypeStruct((nc, n_head, 2 * q_len), f32),
                   jax.ShapeDtypeStruct((nc, n_head, 2 * q_len), f32)),
        grid=grid,
        in_specs=[cur_spec, halo_spec, prev_spec, _full_spec((CONV_W, cdim)), _full_spec((1, cdim)),
                  _row_chunk_spec(rb0, n_chunk, q_len, n_head), _full_spec((1, n_head)), _full_spec((1, n_head))],
        out_specs=(_per_chunk_spec(n_chunk, (n_pair, q_len, LANES)), _per_chunk_spec(n_chunk, (n_grp, q_len, MB_STATE)),
                   _per_chunk_spec(n_chunk, (n_grp, q_len, MB_STATE)), _per_chunk_spec(n_chunk, (q_len, n_head)),
                   _per_chunk_spec(n_chunk, (q_len, n_head)), _per_chunk_spec(n_chunk, (n_head, 2 * q_len)),
                   _per_chunk_spec(n_chunk, (n_head, 2 * q_len))),
        scratch_shapes=[pltpu.VMEM((SUBLANES + q_len, cdim), f32)],
        compiler_params=_cparams("parallel", "arbitrary"),
    )(xbc, xbc, prev8, p['mb_conv_w'], p['mb_conv_b'].reshape(1, cdim), dt_raw,
      p['mb_dt_bias'].reshape(1, n_head), p['mb_a_log'].reshape(1, n_head))

    d_lanes = jnp.repeat(p['mb_d'], MB_HEADDIM).reshape(1, n_pair, LANES)
    y, s_fin = pl.pallas_call(
        functools.partial(_ssd_chunk_kernel, q_len=q_len, n_pair=n_pair, n_grp=n_grp),
        out_shape=(jax.ShapeDtypeStruct((nc * q_len, MB_DIM), jnp.bfloat16),
                   jax.ShapeDtypeStruct((n_seq, n_pair, LANES, MB_STATE), f32)),
        grid=grid,
        in_specs=[_per_chunk_spec(n_chunk, (n_pair, q_len, LANES)), _per_chunk_spec(n_chunk, (n_grp, q_len, MB_STATE)),
                  _per_chunk_spec(n_chunk, (n_grp, q_len, MB_STATE)), _per_chunk_spec(n_chunk, (q_len, n_head)),
                  _per_chunk_spec(n_chunk, (q_len, n_head)), _per_chunk_spec(n_chunk, (n_head, 2 * q_len)),
                  _per_chunk_spec(n_chunk, (n_head, 2 * q_len)), _row_chunk_spec(rb0, n_chunk, q_len, MB_DIM),
                  _full_spec((1, n_pair, LANES)), _full_spec((1, MB_DIM)), _per_seq_spec((n_pair, LANES, MB_STATE))],
        out_specs=(_row_chunk_spec(0, n_chunk, q_len, MB_DIM), _per_seq_spec((n_pair, LANES, MB_STATE))),
        scratch_shapes=[pltpu.VMEM((n_pair, q_len, LANES), f32)],
        compiler_params=_cparams("parallel", "arbitrary"),
    )(xh, bh, ch, cum, dt, cumt, dtt, z, d_lanes, p['mb_norm'].reshape(1, MB_DIM), s0)
    return y, s_fin


def _s5_step(h_re, h_im, a_re, a_im, bu_re, bu_im):
    return a_re * h_re - a_im * h_im + bu_re, a_re * h_im + a_im * h_re + bu_im


def _s5_pass1_kernel(u_ref, bre_ref, bim_ref, are_ref, aim_ref, vre_ref, vim_ref):
    a_re, a_im = are_ref[...], aim_ref[...]
    b_re, b_im = bre_ref[0], bim_ref[0]
    h_re = _dot(u_ref[0], b_re)
    h_im = _dot(u_ref[0], b_im)
    for t in range(1, S5_SUB):
        h_re, h_im = _s5_step(h_re, h_im, a_re, a_im, _dot(u_ref[t], b_re), _dot(u_ref[t], b_im))
    vre_ref[...] = h_re
    vim_ref[...] = h_im


def _s5_pass2_kernel(u_ref, hre_ref, him_ref, bre_ref, bim_ref, cre_ref, cim_ref, are_ref, aim_ref, d_ref, y_ref):
    a_re, a_im = are_ref[...], aim_ref[...]
    b_re, b_im = bre_ref[0], bim_ref[0]
    c_re, c_im = cre_ref[0], cim_ref[0]
    h_re, h_im = hre_ref[...], him_ref[...]
    for t in range(S5_SUB):
        u = u_ref[t]
        h_re, h_im = _s5_step(h_re, h_im, a_re, a_im, _dot(u, b_re), _dot(u, b_im))
        y_ref[t] = _dot(h_re, c_re) - _dot(h_im, c_im) + d_ref[...] * u


def _s5_carry_kernel(vre_ref, vim_ref, are_ref, aim_ref, s0re_ref, s0im_ref,
                     hre_ref, him_ref, pre_ref, pim_ref, sre_ref, sim_ref, *, n_main, n_dec):
    a_re, a_im = are_ref[...], aim_ref[...]
    half = SUBLANES // 2
    meta = pl.ds(n_main, SUBLANES)
    row = lax.broadcasted_iota(jnp.int32, (SUBLANES, vre_ref.shape[1]), 0)
    first_half = row < half

    def cmul_add(h_re, h_im, v_re, v_im):
        return a_re * h_re - a_im * h_im + v_re, a_re * h_im + a_im * h_re + v_im

    def two_pieces(i, carry):
        h_re, h_im = carry
        rows = pl.ds(pl.multiple_of(i * SUBLANES, SUBLANES), SUBLANES)
        v_re, v_im = vre_ref[rows, :], vim_ref[rows, :]
        t_re, t_im = cmul_add(h_re, h_im, v_re, v_im)
        t_re = pltpu.roll(t_re, half, 0)
        t_im = pltpu.roll(t_im, half, 0)
        hre_ref[rows, :] = jnp.where(first_half, h_re, t_re)
        him_ref[rows, :] = jnp.where(first_half, h_im, t_im)
        n_re, n_im = cmul_add(t_re, t_im, v_re, v_im)
        return pltpu.roll(n_re, half, 0), pltpu.roll(n_im, half, 0)

    h_re, h_im = lax.fori_loop(0, n_main // SUBLANES, two_pieces, (vre_ref[meta, :], vim_ref[meta, :]))
    pre_ref[...] = h_re
    pim_ref[...] = h_im
    hre_ref[meta, :] = jnp.zeros((SUBLANES, vre_ref.shape[1]), jnp.float32)
    him_ref[meta, :] = jnp.zeros((SUBLANES, vre_ref.shape[1]), jnp.float32)
    dec = pl.ds(n_main + SUBLANES, n_dec)
    s0_re, s0_im = s0re_ref[...], s0im_ref[...]
    hre_ref[dec, :] = s0_re
    him_ref[dec, :] = s0_im
    f_re, f_im = cmul_add(s0_re, s0_im, vre_ref[dec, :], vim_ref[dec, :])
    sre_ref[...] = f_re
    sim_ref[...] = f_im
    tail = pl.ds(n_main + SUBLANES + n_dec, SUBLANES)
    hre_ref[tail, :] = jnp.zeros((SUBLANES, vre_ref.shape[1]), jnp.float32)
    him_ref[tail, :] = jnp.zeros((SUBLANES, vre_ref.shape[1]), jnp.float32)


def _s5_post_kernel(y_ref, w_ref, b_ref, nw_ref, o_ref, wbf_ref):
    @pl.when(pl.program_id(0) == 0)
    def _():
        wbf_ref[...] = w_ref[...].astype(jnp.bfloat16)

    gl = jax.nn.gelu(y_ref[...])
    gate = jnp.dot(gl.astype(jnp.bfloat16), wbf_ref[...], preferred_element_type=jnp.float32) + b_ref[...]
    o_ref[...] = _rms(gl * jax.nn.sigmoid(gate), nw_ref[...]).astype(o_ref.dtype)


def _s5_discretize(p):
    lam_re, lam_im = p['s5_lambda_re'], p['s5_lambda_im']
    step = jnp.exp(p['s5_log_step'])[:, None]
    mag = jnp.exp(lam_re * step)
    a_re = mag * jnp.cos(lam_im * step)
    a_im = mag * jnp.sin(lam_im * step)
    den = lam_re * lam_re + lam_im * lam_im
    f_re = ((a_re - 1.0) * lam_re + a_im * lam_im) / den
    f_im = (a_im * lam_re - (a_re - 1.0) * lam_im) / den
    bb_re = f_re[..., None] * p['s5_b_re'] - f_im[..., None] * p['s5_b_im']
    bb_im = f_re[..., None] * p['s5_b_im'] + f_im[..., None] * p['s5_b_re']
    return a_re, a_im, bb_re, bb_im


def _block_diag(w):
    g, r, c = w.shape
    nb = g // S5_GB
    eye = jnp.eye(S5_GB, dtype=w.dtype)
    return jnp.einsum('bgrc,gh->bgrhc', w.reshape(nb, S5_GB, r, c), eye).reshape(nb, S5_GB * r, S5_GB * c)


def s5_mixer_rows(u, s0_re, s0_im, p, *, n_prompt, seq, n_dec):
    f32 = jnp.float32
    n_main = n_prompt * seq
    n_meta = n_prompt * N_META
    assert n_prompt == SUBLANES // 2 and seq % (2 * S5_SUB) == 0 and n_dec % SUBLANES == 0
    pieces = seq // S5_SUB
    n_main_p = pieces * n_prompt
    n_piece = n_main_p + SUBLANES + n_dec + SUBLANES
    assert n_piece % S5_ROWS == 0
    a_re, a_im, bb_re, bb_im = _s5_discretize(p)
    sdim = S5_GROUPS * S5_STATE
    nb = S5_GROUPS // S5_GB
    gw = S5_GB * S5_STATE
    uw = S5_GB * S5_GROUP
    b_re = _block_diag(jnp.swapaxes(bb_re, 1, 2))
    b_im = _block_diag(jnp.swapaxes(bb_im, 1, 2))
    c_re = _block_diag(jnp.swapaxes(p['s5_c_re'], 1, 2))
    c_im = _block_diag(jnp.swapaxes(p['s5_c_im'], 1, 2))
    a16_re, a16_im = a_re, a_im
    for _ in range(int(math.log2(S5_SUB))):
        a16_re, a16_im = a16_re * a16_re - a16_im * a16_im, 2.0 * a16_re * a16_im
    flat = lambda a: a.reshape(1, sdim)

    um = u[:n_main].reshape(n_prompt, pieces, S5_SUB, S5_DIM).transpose(2, 1, 0, 3).reshape(S5_SUB, n_main_p, S5_DIM)
    ue = u[n_main:n_main + n_meta].reshape(n_prompt, S5_SUB, S5_DIM).transpose(1, 0, 2)
    us = u[n_main + n_meta:].reshape(n_dec, S5_SUB, S5_DIM).transpose(1, 0, 2)
    zpad = jnp.zeros((S5_SUB, SUBLANES - n_prompt, S5_DIM), f32)
    u4 = jnp.concatenate([um, ue, zpad, us, jnp.zeros((S5_SUB, SUBLANES, S5_DIM), f32)], axis=1)

    grid = (nb, n_piece // S5_ROWS)
    u_spec = pl.BlockSpec((S5_SUB, S5_ROWS, uw), lambda b, r: (0, r, b))
    st_spec = pl.BlockSpec((S5_ROWS, gw), lambda b, r: (r, b))
    bw_spec = pl.BlockSpec((1, uw, gw), lambda b, r: (b, 0, 0))
    cw_spec = pl.BlockSpec((1, gw, uw), lambda b, r: (b, 0, 0))
    a_spec = pl.BlockSpec((1, gw), lambda b, r: (0, b))
    v_re, v_im = pl.pallas_call(
        _s5_pass1_kernel,
        out_shape=(jax.ShapeDtypeStruct((n_piece, sdim), f32), jax.ShapeDtypeStruct((n_piece, sdim), f32)),
        grid=grid, in_specs=[u_spec, bw_spec, bw_spec, a_spec, a_spec], out_specs=(st_spec, st_spec),
        compiler_params=_cparams("parallel", "parallel"),
    )(u4, b_re, b_im, flat(a_re), flat(a_im))

    lanes_blk = 1024
    col = lambda rows: pl.BlockSpec((rows, lanes_blk), lambda j: (0, j))
    h_re, h_im, p_re, p_im, s_re, s_im = pl.pallas_call(
        functools.partial(_s5_carry_kernel, n_main=n_main_p, n_dec=n_dec),
        out_shape=(jax.ShapeDtypeStruct((n_piece, sdim), f32), jax.ShapeDtypeStruct((n_piece, sdim), f32),
                   jax.ShapeDtypeStruct((SUBLANES, sdim), f32), jax.ShapeDtypeStruct((SUBLANES, sdim), f32),
                   jax.ShapeDtypeStruct((n_dec, sdim), f32), jax.ShapeDtypeStruct((n_dec, sdim), f32)),
        grid=(sdim // lanes_blk,),
        in_specs=[col(n_piece), col(n_piece), col(1), col(1), col(n_dec), col(n_dec)],
        out_specs=(col(n_piece), col(n_piece), col(SUBLANES), col(SUBLANES), col(n_dec), col(n_dec)),
        compiler_params=_cparams("parallel"),
    )(v_re, v_im, flat(a16_re), flat(a16_im), s0_re, s0_im)

    y4 = pl.pallas_call(
        _s5_pass2_kernel,
        out_shape=jax.ShapeDtypeStruct((S5_SUB, n_piece, S5_DIM), f32),
        grid=grid,
        in_specs=[u_spec, st_spec, st_spec, bw_spec, bw_spec, cw_spec, cw_spec, a_spec, a_spec,
                  pl.BlockSpec((1, uw), lambda b, r: (0, b))],
        out_specs=u_spec,
        compiler_params=_cparams("parallel", "parallel"),
    )(u4, h_re, h_im, b_re, b_im, c_re, c_im, flat(a_re), flat(a_im), p['s5_d'].reshape(1, S5_DIM))

    ym = y4[:, :n_main_p].reshape(S5_SUB, pieces, n_prompt, S5_DIM).transpose(2, 1, 0, 3).reshape(n_main, S5_DIM)
    ye = y4[:, n_main_p:n_main_p + n_prompt].transpose(1, 0, 2).reshape(n_meta, S5_DIM)
    ys = y4[:, n_main_p + SUBLANES:n_main_p + SUBLANES + n_dec].transpose(1, 0, 2).reshape(n_dec * S5_SUB, S5_DIM)
    y = jnp.concatenate([ym, ye, ys], axis=0)

    m = y.shape[0]
    mixed = pl.pallas_call(
        _s5_post_kernel,
        out_shape=jax.ShapeDtypeStruct((m, S5_DIM), jnp.bfloat16),
        grid=(m // ROW_TILE,),
        in_specs=[_row_spec(S5_DIM),
                  pl.BlockSpec((S5_DIM, S5_DIM), lambda i: (0, 0), pipeline_mode=pl.Buffered(1)),
                  _full_spec((1, S5_DIM)), _full_spec((1, S5_DIM))],
        out_specs=_row_spec(S5_DIM),
        scratch_shapes=[pltpu.VMEM((S5_DIM, S5_DIM), jnp.bfloat16)],
        compiler_params=_cparams("arbitrary"),
    )(y, p['s5_glu_w'], p['s5_glu_b'].reshape(1, S5_DIM), p['s5_norm'].reshape(1, S5_DIM))
    return mixed, p_re[:n_prompt], p_im[:n_prompt], s_re, s_im


def _conv_history(x, cache, n_prompt, seq, n_dec, dec_len):
    n_main = n_prompt * seq
    n_meta = n_prompt * N_META
    k = CONV_W - 1
    cdim = x.shape[1]
    pad = ((0, 0), (SUBLANES - k, 0), (0, 0))
    meta = x[n_main:n_main + n_meta].reshape(n_prompt, N_META, cdim)
    prev_short = jnp.concatenate([jnp.zeros((n_prompt, SUBLANES, cdim), x.dtype), jnp.pad(cache, pad)], axis=0)
    prev_main = jnp.pad(meta[:, N_META - k:], pad)
    new_prompt = x[:n_main].reshape(n_prompt, seq, cdim)[:, seq - k:]
    new_sample = x[n_main + n_meta:].reshape(n_dec, dec_len, cdim)[:, dec_len - k:]
    return prev_short, prev_main, new_prompt, new_sample


def _two_groups(group_fn, prev_short, prev_main, s0_sample, n_prompt, seq, n_dec, dec_len):
    n_main = n_prompt * seq
    s0_short = jnp.concatenate([jnp.zeros((n_prompt,) + s0_sample.shape[1:], s0_sample.dtype), s0_sample], axis=0)
    y_short, s_short = group_fn(prev_short, s0_short, row0=n_main, n_seq=n_prompt + n_dec, n_chunk=1, q_len=N_META)
    y_main, s_main = group_fn(prev_main, s_short[:n_prompt], row0=0, n_seq=n_prompt, n_chunk=seq // CHUNK,
                              q_len=CHUNK)
    return jnp.concatenate([y_main, y_short], axis=0), s_main, s_short[n_prompt:]


def kernel(x_prompt, x_sample, state_s5_re, state_s5_im, state_ssd, cache_ssd_conv, state_gdn, cache_gdn_conv, meta_tokens, norm_mix0, w_in0, s5_lambda_re, s5_lambda_im, s5_log_step, s5_b_re, s5_b_im, s5_c_re, s5_c_im, s5_d, s5_glu_w, s5_glu_b, s5_norm, mb_conv_w, mb_conv_b, mb_dt_bias, mb_a_log, mb_d, mb_norm, w_out0, norm_ffn0, ffn_w_gate, ffn_w_up, ffn_w_down, norm_mix1, w_in1, gdn_conv_w, gdn_a_log, gdn_dt_bias, gdn_norm, w_out1, norm_ffn1, moe_router, moe_w_gate, moe_w_up, moe_w_down, norm_final):
    p = dict(s5_lambda_re=s5_lambda_re, s5_lambda_im=s5_lambda_im, s5_log_step=s5_log_step, s5_b_re=s5_b_re,
             s5_b_im=s5_b_im, s5_c_re=s5_c_re, s5_c_im=s5_c_im, s5_d=s5_d, s5_glu_w=s5_glu_w, s5_glu_b=s5_glu_b,
             s5_norm=s5_norm, mb_conv_w=mb_conv_w, mb_conv_b=mb_conv_b, mb_dt_bias=mb_dt_bias, mb_a_log=mb_a_log,
             mb_d=mb_d, mb_norm=mb_norm, gdn_conv_w=gdn_conv_w, gdn_a_log=gdn_a_log, gdn_dt_bias=gdn_dt_bias,
             gdn_norm=gdn_norm)
    n_prompt, seq, d = x_prompt.shape
    n_dec, dec_len, _ = x_sample.shape
    assert dec_len == N_META and seq % CHUNK == 0
    n_main = n_prompt * seq
    n_meta = n_prompt * N_META
    groups = (n_prompt, seq, n_dec, dec_len)

    x0 = jnp.concatenate([x_prompt.reshape(n_main, d), jnp.tile(meta_tokens, (n_prompt, 1)),
                          x_sample.reshape(n_dec * dec_len, d)], axis=0)

    h = rms_norm_rows(x0, norm_mix0, jnp.bfloat16)
    o1 = S5_DIM
    o2 = o1 + MB_DIM
    o3 = o2 + MB_CONV_DIM
    u = matmul_rows(h, w_in0, col0=0, ncols=S5_DIM)
    z0 = matmul_rows(h, w_in0, col0=o1, ncols=MB_DIM)
    xbc = matmul_rows(h, w_in0, col0=o2, ncols=MB_CONV_DIM)
    dt_raw = matmul_rows(h, w_in0[:, o3:])

    y_s5, p_s5_re, p_s5_im, s_s5_re, s_s5_im = s5_mixer_rows(
        u, state_s5_re.reshape(n_dec, -1), state_s5_im.reshape(n_dec, -1), p, n_prompt=n_prompt, seq=seq, n_dec=n_dec)
    s5_shape = (S5_GROUPS, S5_STATE)

    prev_short, prev_main, p_ssd_conv, s_ssd_conv = _conv_history(xbc, cache_ssd_conv, *groups)
    ssd_fn = functools.partial(ssd_group, xbc, z0, dt_raw)
    y_mb, p_ssd, s_ssd = _two_groups(
        lambda prev8, s0, **kw: ssd_fn(prev8, s0, p, **kw), prev_short, prev_main,
        state_ssd.reshape(n_dec, MB_HEADS // 2, LANES, MB_STATE), *groups)
    ssd_shape = (MB_HEADS, MB_HEADDIM, MB_STATE)

    mixed = jnp.concatenate([y_s5, y_mb], axis=1)
    x1 = matmul_rows(mixed, w_out0, tn=256, residual=x0)
    h = rms_norm_rows(x1, norm_ffn0, jnp.bfloat16)
    ffn = dense_swiglu(h, ffn_w_gate, ffn_w_up, ffn_w_down)

    x2, h = add_rms_norm_rows(x1, ffn, norm_mix1)
    o1 = GDN_CONV_DIM
    o2 = o1 + GDN_V_DIM
    o3 = o2 + GDN_V_HEADS
    qkv = matmul_rows(h, w_in1, col0=0, ncols=GDN_CONV_DIM)
    z1 = matmul_rows(h, w_in1, col0=o1, ncols=GDN_V_DIM)
    b_raw = matmul_rows(h, w_in1[:, o2:o3])
    a_raw = matmul_rows(h, w_in1[:, o3:])
    prev_short, prev_main, p_gdn_conv, s_gdn_conv = _conv_history(qkv, cache_gdn_conv, *groups)
    gdn_fn = functools.partial(gdn_group, qkv, z1, b_raw, a_raw)
    y_gdn, p_gdn, s_gdn = _two_groups(
        lambda prev8, s0, **kw: gdn_fn(prev8, s0, p, **kw), prev_short, prev_main, state_gdn, *groups)

    x3 = matmul_rows(y_gdn, w_out1, tn=256, residual=x2)
    h, logits = rms_norm_router_rows(x3, norm_ffn1, moe_router)
    moe = moe_swiglu_rows(h, logits, moe_w_gate, moe_w_up, moe_w_down)
    y = add_rms_norm_only_rows(x3, moe, norm_final, jnp.float32)

    y_prompt = y[:n_main].reshape(n_prompt, seq, d)
    y_sample = y[n_main + n_meta:].reshape(n_dec, dec_len, d)
    return (y_prompt, y_sample,
            p_s5_re.reshape((n_prompt,) + s5_shape), p_s5_im.reshape((n_prompt,) + s5_shape),
            p_ssd.reshape((n_prompt,) + ssd_shape), p_ssd_conv, p_gdn, p_gdn_conv,
            s_s5_re.reshape((n_dec,) + s5_shape), s_s5_im.reshape((n_dec,) + s5_shape),
            s_ssd.reshape((n_dec,) + ssd_shape), s_ssd_conv, s_gdn, s_gdn_conv)
```

```python
import functools
import math

import jax
import jax.numpy as jnp
from jax import lax
from jax.experimental import pallas as pl
from jax.experimental.pallas import tpu as pltpu

D_MODEL = 4096
CHUNK = 64
N_META = 16
NORM_EPS = 1e-6
CONV_W = 4

S5_GROUP = 16
S5_DIM = D_MODEL // 2
S5_GROUPS = S5_DIM // S5_GROUP
S5_STATE = 64
MB_HEADDIM = 64
MB_DIM = D_MODEL
MB_HEADS = MB_DIM // MB_HEADDIM
MB_GROUPS = 8
MB_STATE = 128
MB_CONV_DIM = MB_DIM + 2 * MB_GROUPS * MB_STATE

GDN_QK_HEADS = 32
GDN_V_HEADS = 64
GDN_DK = 128
GDN_DV = 128
GDN_QK_DIM = GDN_QK_HEADS * GDN_DK
GDN_V_DIM = GDN_V_HEADS * GDN_DV
GDN_CONV_DIM = 2 * GDN_QK_DIM + GDN_V_DIM

N_EXPERTS = 8
TOP_K = 2

V7X_VMEM_LIMIT_BYTES = 56 * 1024 * 1024
MXU_COLS = 256
LANES = 128
SUBLANES = 8
ROW_TILE = 224
MM_ROWS = 1216
FFN_ROWS = 1024
FFN_SUB = 256
FFN_TF = MXU_COLS
GDN_PACK = 4
S5_SUB = 16
S5_GB = MXU_COLS // S5_GROUP
S5_ROWS = 272

_HI = lax.Precision.HIGHEST


def _cparams(*sem):
    return pltpu.CompilerParams(dimension_semantics=sem, vmem_limit_bytes=V7X_VMEM_LIMIT_BYTES)


def _dot(a, b):
    return jnp.dot(a, b, precision=_HI, preferred_element_type=jnp.float32)


def _dot_nt(a, b):
    return lax.dot_general(a, b, (((1,), (1,)), ((), ())), precision=_HI, preferred_element_type=jnp.float32)


def _dot_tn(a, b):
    return lax.dot_general(a, b, (((0,), (0,)), ((), ())), precision=_HI, preferred_element_type=jnp.float32)


def _bdot(a, b):
    return jnp.dot(a.astype(jnp.bfloat16), b.astype(jnp.bfloat16), preferred_element_type=jnp.float32)


def _bdot_nt(a, b):
    return lax.dot_general(a.astype(jnp.bfloat16), b.astype(jnp.bfloat16), (((1,), (1,)), ((), ())),
                           preferred_element_type=jnp.float32)


def _bdot_tn(a, b):
    return lax.dot_general(a.astype(jnp.bfloat16), b.astype(jnp.bfloat16), (((0,), (0,)), ((), ())),
                           preferred_element_type=jnp.float32)


def _silu(x):
    return x * jax.nn.sigmoid(x)


def _softplus(x):
    return jnp.maximum(x, 0.0) + jnp.log1p(jnp.exp(-jnp.abs(x)))


def _rms(x, w):
    ms = jnp.mean(x * x, axis=-1, keepdims=True)
    return (x * lax.rsqrt(ms + NORM_EPS)) * w


def _norm_kernel(x_ref, w_ref, h_ref):
    h_ref[...] = _rms(x_ref[...], w_ref[...]).astype(h_ref.dtype)


def _add_norm_kernel(x_ref, d_ref, w_ref, xo_ref, h_ref):
    x = x_ref[...] + d_ref[...]
    xo_ref[...] = x
    h_ref[...] = _rms(x, w_ref[...]).astype(h_ref.dtype)


def _add_norm_only_kernel(x_ref, d_ref, w_ref, h_ref):
    h_ref[...] = _rms(x_ref[...] + d_ref[...], w_ref[...]).astype(h_ref.dtype)


def _norm_router_kernel(x_ref, w_ref, wr_ref, h_ref, lg_ref):
    h = _rms(x_ref[...], w_ref[...])
    h_ref[...] = h.astype(h_ref.dtype)
    lg_ref[...] = _dot(h, wr_ref[...])


def _row_spec(cols):
    return pl.BlockSpec((ROW_TILE, cols), lambda i: (i, 0))


def _full_spec(shape):
    return pl.BlockSpec(shape, lambda *_: (0,) * len(shape))


def rms_norm_rows(x, w, out_dtype):
    m, d = x.shape
    return pl.pallas_call(
        _norm_kernel,
        out_shape=jax.ShapeDtypeStruct((m, d), out_dtype),
        grid=(m // ROW_TILE,),
        in_specs=[_row_spec(d), _full_spec((1, d))],
        out_specs=_row_spec(d),
        compiler_params=_cparams("parallel"),
    )(x, w.reshape(1, d))


def add_rms_norm_rows(x, delta, w):
    m, d = x.shape
    return pl.pallas_call(
        _add_norm_kernel,
        out_shape=(jax.ShapeDtypeStruct((m, d), jnp.float32), jax.ShapeDtypeStruct((m, d), jnp.bfloat16)),
        grid=(m // ROW_TILE,),
        in_specs=[_row_spec(d), _row_spec(d), _full_spec((1, d))],
        out_specs=(_row_spec(d), _row_spec(d)),
        compiler_params=_cparams("parallel"),
    )(x, delta, w.reshape(1, d))


def add_rms_norm_only_rows(x, delta, w, out_dtype):
    m, d = x.shape
    return pl.pallas_call(
        _add_norm_only_kernel,
        out_shape=jax.ShapeDtypeStruct((m, d), out_dtype),
        grid=(m // ROW_TILE,),
        in_specs=[_row_spec(d), _row_spec(d), _full_spec((1, d))],
        out_specs=_row_spec(d),
        compiler_params=_cparams("parallel"),
    )(x, delta, w.reshape(1, d))


def rms_norm_router_rows(x, w, w_router):
    m, d = x.shape
    e = w_router.shape[1]
    return pl.pallas_call(
        _norm_router_kernel,
        out_shape=(jax.ShapeDtypeStruct((m, d), jnp.bfloat16), jax.ShapeDtypeStruct((m, e), jnp.float32)),
        grid=(m // ROW_TILE,),
        in_specs=[_row_spec(d), _full_spec((1, d)), _full_spec((d, e))],
        out_specs=(_row_spec(d), _row_spec(e)),
        compiler_params=_cparams("parallel"),
    )(x, w.reshape(1, d), w_router)


def _mm_kernel(x_ref, w_ref, o_ref):
    o_ref[...] = jnp.dot(x_ref[...], w_ref[...].astype(jnp.bfloat16),
                         preferred_element_type=jnp.float32).astype(o_ref.dtype)


def _mm_res_kernel(x_ref, w_ref, r_ref, o_ref):
    o_ref[...] = r_ref[...] + jnp.dot(x_ref[...], w_ref[...].astype(jnp.bfloat16),
                                      preferred_element_type=jnp.float32)


def matmul_rows(x, w, *, col0=0, ncols=None, tn=512, residual=None, out_dtype=jnp.float32):
    m, k = x.shape
    ncols = w.shape[1] - col0 if ncols is None else ncols
    if ncols < tn:
        tn = ncols
    assert m % MM_ROWS == 0 and ncols % tn == 0 and col0 % tn == 0
    cb = col0 // tn
    x_spec = pl.BlockSpec((MM_ROWS, k), lambda i, j: (i, 0), pipeline_mode=pl.Buffered(1))
    w_spec = pl.BlockSpec((k, tn), lambda i, j: (0, j + cb))
    o_spec = pl.BlockSpec((MM_ROWS, tn), lambda i, j: (i, j))
    grid = (m // MM_ROWS, ncols // tn)
    if residual is None:
        return pl.pallas_call(
            _mm_kernel, out_shape=jax.ShapeDtypeStruct((m, ncols), out_dtype), grid=grid,
            in_specs=[x_spec, w_spec], out_specs=o_spec,
            compiler_params=_cparams("parallel", "arbitrary"))(x, w)
    return pl.pallas_call(
        _mm_res_kernel, out_shape=jax.ShapeDtypeStruct((m, ncols), jnp.float32), grid=grid,
        in_specs=[x_spec, w_spec, o_spec], out_specs=o_spec,
        compiler_params=_cparams("parallel", "arbitrary"))(x, w, residual)


def _swiglu_kernel(ce_ref, cv_ref, x_ref, wg_ref, wu_ref, wd_ref, o_ref):
    c = pl.program_id(0)
    f = pl.program_id(1)
    n_valid = cv_ref[c]

    @pl.when(f == 0)
    def _():
        o_ref[...] = jnp.zeros_like(o_ref)

    @pl.when(n_valid > 0)
    def _():
        wg = wg_ref[...].astype(jnp.bfloat16)
        wu = wu_ref[...].astype(jnp.bfloat16)
        wd = wd_ref[...].astype(jnp.bfloat16)
        for s in range(FFN_ROWS // FFN_SUB):
            @pl.when(s < n_valid)
            def _():
                rows = pl.ds(s * FFN_SUB, FFN_SUB)
                xs = x_ref[rows, :]
                g = jnp.dot(xs, wg, preferred_element_type=jnp.float32)
                u = jnp.dot(xs, wu, preferred_element_type=jnp.float32)
                h = (_silu(g) * u).astype(jnp.bfloat16)
                o_ref[rows, :] += jnp.dot(h, wd, preferred_element_type=jnp.float32)


def swiglu_chunks(xs, wg, wu, wd, chunk_expert, chunk_valid):
    rows, d = xs.shape
    n_chunks = rows // FFN_ROWS
    ff = wg.shape[2]
    nf = ff // FFN_TF

    def f_idx(c, f, cv):
        return jnp.where(cv[c] > 0, f, nf - 1)

    grid_spec = pltpu.PrefetchScalarGridSpec(
        num_scalar_prefetch=2,
        grid=(n_chunks, nf),
        in_specs=[
            pl.BlockSpec((FFN_ROWS, d), lambda c, f, ce, cv: (c, 0), pipeline_mode=pl.Buffered(1)),
            pl.BlockSpec((None, d, FFN_TF), lambda c, f, ce, cv: (ce[c], 0, f_idx(c, f, cv))),
            pl.BlockSpec((None, d, FFN_TF), lambda c, f, ce, cv: (ce[c], 0, f_idx(c, f, cv))),
            pl.BlockSpec((None, FFN_TF, d), lambda c, f, ce, cv: (ce[c], f_idx(c, f, cv), 0)),
        ],
        out_specs=pl.BlockSpec((FFN_ROWS, d), lambda c, f, ce, cv: (c, 0), pipeline_mode=pl.Buffered(1)),
    )
    return pl.pallas_call(
        _swiglu_kernel,
        out_shape=jax.ShapeDtypeStruct((rows, d), jnp.float32),
        grid_spec=grid_spec,
        compiler_params=_cparams("parallel", "arbitrary"),
    )(chunk_expert, chunk_valid, xs, wg, wu, wd)


def dense_swiglu(h, wg, wu, wd):
    m, d = h.shape
    n_chunks = pl.cdiv(m, FFN_ROWS)
    pad = n_chunks * FFN_ROWS - m
    xs = jnp.pad(h, ((0, pad), (0, 0)))
    starts = jnp.arange(n_chunks, dtype=jnp.int32) * FFN_ROWS
    valid = (jnp.clip(m - starts, 0, FFN_ROWS) + FFN_SUB - 1) // FFN_SUB
    out = swiglu_chunks(xs, wg[None], wu[None], wd[None], jnp.zeros((n_chunks,), jnp.int32),
                        valid.astype(jnp.int32))
    return out[:m]


def moe_swiglu_rows(h, logits, wg, wu, wd):
    m, d = h.shape
    top_v, top_i = lax.top_k(logits, TOP_K)
    top_w = jax.nn.softmax(top_v, axis=-1)
    n_pairs = m * TOP_K
    max_chunks = (n_pairs + N_EXPERTS * (FFN_ROWS - 1)) // FFN_ROWS
    flat_e = top_i.reshape(-1).astype(jnp.int32)
    order = jnp.argsort(flat_e, stable=True).astype(jnp.int32)
    counts = jnp.sum(jax.nn.one_hot(flat_e, N_EXPERTS, dtype=jnp.int32), axis=0)
    chunks_e = (counts + FFN_ROWS - 1) // FFN_ROWS
    chunk_end = jnp.cumsum(chunks_e)
    chunk_start = chunk_end - chunks_e
    pair_start = jnp.cumsum(counts) - counts
    sorted_e = flat_e[order]
    rank = jnp.arange(n_pairs, dtype=jnp.int32) - pair_start[sorted_e]
    dest_sorted = chunk_start[sorted_e] * FFN_ROWS + rank
    src_row = jnp.zeros((max_chunks * FFN_ROWS,), jnp.int32).at[dest_sorted].set(order // TOP_K)
    pos = jnp.zeros((n_pairs,), jnp.int32).at[order].set(dest_sorted).reshape(m, TOP_K)

    cidx = jnp.arange(max_chunks, dtype=jnp.int32)
    total = chunk_end[-1]
    ce = jnp.sum((cidx[:, None] >= chunk_end[None, :]).astype(jnp.int32), axis=1)
    last_e = jnp.sum((total - 1 >= chunk_end).astype(jnp.int32))
    ce = jnp.where(cidx < total, ce, last_e).astype(jnp.int32)
    rows_in = jnp.clip(counts[ce] - (cidx - chunk_start[ce]) * FFN_ROWS, 0, FFN_ROWS)
    cv = jnp.where(cidx < total, (rows_in + FFN_SUB - 1) // FFN_SUB, 0).astype(jnp.int32)

    xs = jnp.take(h, src_row, axis=0)
    ys = swiglu_chunks(xs, wg, wu, wd, ce, cv)
    y2 = jnp.take(ys, pos.reshape(-1), axis=0).reshape(m, TOP_K, d)
    return y2[:, 0] * top_w[:, 0:1] + y2[:, 1] * top_w[:, 1:2]


def _conv_rows_into(xp_ref, cur_ref, halo_ref, prev_ref, first_chunk, q_len):
    @pl.when(first_chunk)
    def _():
        xp_ref[0:SUBLANES, :] = prev_ref[0]

    @pl.when(jnp.logical_not(first_chunk))
    def _():
        xp_ref[0:SUBLANES, :] = halo_ref[...]

    xp_ref[SUBLANES:SUBLANES + q_len, :] = cur_ref[...]


def _conv_tile(xp_ref, w_ref, lanes, q_len):
    base = SUBLANES - (CONV_W - 1)
    acc = xp_ref[base:base + q_len, lanes] * w_ref[0:1, lanes]
    for j in range(1, CONV_W):
        acc = acc + xp_ref[base + j:base + j + q_len, lanes] * w_ref[j:j + 1, lanes]
    return acc


def _chunk_specs(row_block0, n_chunk, q_len, cols):
    per8 = q_len // SUBLANES

    def cur(s, c):
        return (row_block0 + s * n_chunk + c, 0)

    def halo(s, c):
        return (jnp.maximum((row_block0 + s * n_chunk + c) * per8 - 1, 0), 0)

    return (pl.BlockSpec((q_len, cols), cur), pl.BlockSpec((SUBLANES, cols), halo),
            pl.BlockSpec((1, SUBLANES, cols), lambda s, c: (s, 0, 0)))


def _row_chunk_spec(row_block0, n_chunk, q_len, cols):
    return pl.BlockSpec((q_len, cols), lambda s, c: (row_block0 + s * n_chunk + c, 0))


def _per_chunk_spec(n_chunk, shape):
    nd = len(shape)
    return pl.BlockSpec((1,) + tuple(shape), lambda s, c: (s * n_chunk + c,) + (0,) * nd)


def _per_seq_spec(shape):
    nd = len(shape)
    return pl.BlockSpec((1,) + tuple(shape), lambda s, c: (s,) + (0,) * nd)


def _iota2(shape):
    return lax.broadcasted_iota(jnp.int32, shape, 0), lax.broadcasted_iota(jnp.int32, shape, 1)


def _gdn_prep_kernel(cur_ref, halo_ref, prev_ref, w_ref, braw_ref, araw_ref, alog_ref, dtb_ref,
                     qh_ref, kh_ref, vh_ref, gt_ref, bt_ref, xp_ref, *, q_len, n_qk, n_v):
    _conv_rows_into(xp_ref, cur_ref, halo_ref, prev_ref, pl.program_id(1) == 0, q_len)

    def tile(t):
        return _silu(_conv_tile(xp_ref, w_ref, slice(t * LANES, (t + 1) * LANES), q_len))

    def l2n(x):
        return x * lax.rsqrt(jnp.sum(x * x, axis=-1, keepdims=True) + 1e-6)

    for h in range(n_qk):
        qh_ref[0, h] = l2n(tile(h)) * (GDN_DK ** -0.5)
    for h in range(n_qk):
        kh_ref[0, h] = l2n(tile(n_qk + h))
    for h in range(n_v):
        vh_ref[0, h] = tile(2 * n_qk + h)

    g = -jnp.exp(alog_ref[...]) * _softplus(araw_ref[...] + dtb_ref[...])
    beta = jax.nn.sigmoid(braw_ref[...])
    ri, ci = _iota2((q_len, q_len))
    gt_ref[0] = _dot_tn(g, (ri <= ci).astype(jnp.float32))
    bt_ref[0] = _dot_tn(beta, (ri == ci).astype(jnp.float32))


def _unit_lower_inverse(a, q_len, ri, ci, eye_f):
    blk = 16
    if q_len <= blk:
        a_d, a_o = a, None
    else:
        same = (ri // blk) == (ci // blk)
        a_d = jnp.where(same, a, 0.0)
        a_o = a - a_d
    p = eye_f + a_d
    x = a_d
    for _ in range(int(math.log2(min(q_len, blk))) - 1):
        x = _dot(x, x)
        p = p + _dot(p, x)
    if a_o is None:
        return p
    y = _dot(p, a_o)
    m = eye_f + y
    for _ in range(int(math.log2(q_len // blk)) - 1):
        y = _dot(y, y)
        m = m + _dot(m, y)
    return _dot(m, p)


def _gdn_chunk_kernel(qh_ref, kh_ref, vh_ref, gt_ref, bt_ref, z_ref, nw_ref, s0_ref, o_ref, s_ref, o_scr,
                      *, q_len, n_qk, rep):
    @pl.when(pl.program_id(1) == 0)
    def _():
        s_ref[...] = s0_ref[...]

    ri, ci = _iota2((q_len, q_len))
    causal = ri >= ci
    strict = ri > ci
    eye = ri == ci
    eye_f = eye.astype(jnp.float32)

    def qk_head(hq, carry):
        q = qh_ref[0, hq]
        k = kh_ref[0, hq]
        kk = _dot_nt(k, k)
        qk = _dot_nt(q, k)
        for r in range(rep):
            h = hq * rep + r
            g_row = gt_ref[0, pl.ds(h, 1), :]
            b_row = bt_ref[0, pl.ds(h, 1), :]
            g_col = jnp.sum(jnp.where(eye, g_row, 0.0), axis=1, keepdims=True)
            b_col = jnp.sum(jnp.where(eye, b_row, 0.0), axis=1, keepdims=True)
            decay = jnp.where(causal, jnp.exp(jnp.where(causal, g_col - g_row, 0.0)), 0.0)
            a = jnp.where(strict, -(b_col * kk) * decay, 0.0)
            t_inv = _unit_lower_inverse(a, q_len, ri, ci, eye_f)
            v = vh_ref[0, h]
            e_col = jnp.exp(g_col)
            uu = _dot(t_inv, v * b_col)
            ww = _dot(t_inv, k * (b_col * e_col))
            s = s_ref[0, h]
            v_new = uu - _dot(ww, s)
            attn = jnp.where(causal, qk * decay, 0.0)
            o = _dot(q * e_col, s) + _dot(attn, v_new)
            g_last = g_row[:, q_len - 1:q_len]
            s_ref[0, h] = s * jnp.exp(g_last) + _dot_tn(k * jnp.exp(g_last - g_col), v_new)
            o_scr[h] = _rms(o, nw_ref[...])
        return carry

    lax.fori_loop(0, n_qk, qk_head, 0)
    for h in range(n_qk * rep):
        lanes = slice(h * LANES, (h + 1) * LANES)
        o_ref[:, lanes] = (o_scr[h] * _silu(z_ref[:, lanes])).astype(o_ref.dtype)


def _gdn_chunk_packed_kernel(qh_ref, kh_ref, vh_ref, gt4_ref, gx_ref, bx_ref, z_ref, nw_ref, s0_ref, o_ref, s_ref,
                             sbd_ref, *, q_len, n_v, rep, n_chunk):
    f32 = jnp.float32
    pack = GDN_PACK
    w = pack * q_len
    wv = pack * GDN_DV
    n_grp = n_v // pack
    qk_per = pack // rep
    c = pl.program_id(1)

    r_w, c_w = _iota2((w, w))
    bd_w = (r_w // q_len) == (c_w // q_len)
    r_v, c_v = _iota2((w, wv))
    bd_v = (r_v // q_len) == (c_v // GDN_DV)
    r_s, c_s = _iota2((wv, wv))
    bd_s = (r_s // GDN_DK) == (c_s // GDN_DV)
    r_k, c_k = _iota2((w, qk_per * GDN_DK))
    bd_k = (r_k // (q_len * rep)) == (c_k // GDN_DK)
    ri, li = _iota2((q_len, w))
    pos = jnp.bitwise_and(li, q_len - 1)
    causal = ri >= pos
    strict = ri > pos
    eye_f = (ri == pos).astype(f32)
    same_blk = (ri // 16) == (pos // 16)

    @pl.when(c == 0)
    def _():
        zero = jnp.zeros((GDN_DK, GDN_DV), f32)
        for g in range(n_grp):
            rows = [jnp.concatenate([s0_ref[0, g * pack + i] if j == i else zero for j in range(pack)], axis=1)
                    for i in range(pack)]
            sbd_ref[g] = jnp.concatenate(rows, axis=0)

    bf16 = jnp.bfloat16
    mask_w = bd_w.astype(f32).astype(bf16)
    mask_v = bd_v.astype(f32).astype(bf16)

    def bd(y16):
        return jnp.concatenate([y16] * pack, axis=0) * mask_w

    def bdv(x16):
        return jnp.concatenate([x16] * pack, axis=0) * mask_v

    def split(x):
        hi = x.astype(bf16)
        return hi, (x - hi.astype(f32)).astype(bf16)

    def dot3(x, y, expand):
        xh, xl = split(x)
        yh, yl = split(y)
        rows = x.shape[0]
        top = jnp.dot(jnp.concatenate([xh, xl], axis=0), expand(yh), preferred_element_type=f32)
        return top[:rows] + top[rows:] + jnp.dot(xh, expand(yl), preferred_element_type=f32)

    def pdot(x, y):
        return dot3(x, y, bd)

    def widen(x):
        return jnp.concatenate([jnp.broadcast_to(x[:, i * q_len:i * q_len + 1], (q_len, GDN_DV))
                                for i in range(pack)], axis=1)

    def group(g, carry):
        lanes_w = pl.ds(pl.multiple_of(g * w, w), w)
        lanes_v = pl.ds(pl.multiple_of(g * wv, wv), wv)
        qs = [qh_ref[0, g * qk_per + j] for j in range(qk_per)]
        ks = [kh_ref[0, g * qk_per + j] for j in range(qk_per)]
        k4 = jnp.concatenate([ks[i // rep] for i in range(pack)], axis=1)
        q4 = jnp.concatenate([qs[i // rep] for i in range(pack)], axis=1)
        v4 = jnp.concatenate([vh_ref[0, g * pack + i] for i in range(pack)], axis=1)
        g_row = gt4_ref[0, pl.ds(g, 1), :]
        g_col = gx_ref[0, :, lanes_w]
        b_col = bx_ref[0, :, lanes_w]
        g_col_v = widen(g_col)
        b_col_v = widen(b_col)
        kcat = jnp.concatenate(ks, axis=1)
        kq = jnp.concatenate([kcat, jnp.concatenate(qs, axis=1)], axis=0)
        k_bd = jnp.where(bd_k, jnp.concatenate([kcat] * pack, axis=0), 0.0)
        kkqk = lax.dot_general(kq.astype(bf16), k_bd.astype(bf16), (((1,), (1,)), ((), ())),
                               preferred_element_type=f32)
        decay = jnp.where(causal, jnp.exp(jnp.where(causal, g_col - g_row, 0.0)), 0.0)
        a = jnp.where(strict, -(b_col * kkqk[:q_len]) * decay, 0.0)
        a_d = jnp.where(same_blk, a, 0.0)
        p = eye_f + a_d
        x = a_d
        for _ in range(3):
            x = pdot(x, x)
            p = p + pdot(p, x)
        y = pdot(p, a - a_d)
        m = eye_f + y
        y = pdot(y, y)
        m = m + pdot(m, y)
        t_inv = pdot(m, p)
        e_v = jnp.exp(g_col_v)
        uu = dot3(t_inv, v4 * b_col_v, bdv)
        ww = dot3(t_inv, k4 * (b_col_v * e_v), bdv)
        s_bd = sbd_ref[g]
        r = jnp.dot(jnp.concatenate([ww, q4 * e_v], axis=0).astype(bf16), s_bd.astype(bf16),
                    preferred_element_type=f32)
        v_new = uu - r[:q_len]
        v_new16 = v_new.astype(bf16)
        attn = jnp.where(causal, kkqk[q_len:] * decay, 0.0)
        o4 = r[q_len:] + jnp.dot(attn.astype(bf16), bdv(v_new16), preferred_element_type=f32)
        g_last = g_col_v[q_len - 1:q_len, :]
        upd = lax.dot_general((k4 * jnp.exp(g_last - g_col_v)).astype(bf16), v_new16, (((0,), (0,)), ((), ())),
                              preferred_element_type=f32)
        sbd_ref[g] = s_bd * jnp.exp(g_last) + jnp.where(bd_s, upd, 0.0)
        o_n = jnp.concatenate([_rms(o4[:, i * GDN_DV:(i + 1) * GDN_DV], nw_ref[...]) for i in range(pack)], axis=1)
        o_ref[:, lanes_v] = (o_n * _silu(z_ref[:, lanes_v])).astype(o_ref.dtype)
        return carry

    lax.fori_loop(0, n_grp, group, 0)

    @pl.when(c == n_chunk - 1)
    def _():
        for g in range(n_grp):
            for i in range(pack):
                s_ref[0, g * pack + i] = sbd_ref[g, i * GDN_DK:(i + 1) * GDN_DK, i * GDN_DV:(i + 1) * GDN_DV]


def gdn_group(qkv, z, b_raw, a_raw, prev8, s0, p, *, row0, n_seq, n_chunk, q_len):
    n_qk, n_v = GDN_QK_HEADS, GDN_V_HEADS
    cdim = qkv.shape[1]
    nc = n_seq * n_chunk
    rb0 = row0 // q_len
    grid = (n_seq, n_chunk)
    cur_spec, halo_spec, prev_spec = _chunk_specs(rb0, n_chunk, q_len, cdim)
    small = _row_chunk_spec(rb0, n_chunk, q_len, n_v)
    f32 = jnp.float32
    qh, kh, vh, gt, bt = pl.pallas_call(
        functools.partial(_gdn_prep_kernel, q_len=q_len, n_qk=n_qk, n_v=n_v),
        out_shape=(jax.ShapeDtypeStruct((nc, n_qk, q_len, GDN_DK), f32),
                   jax.ShapeDtypeStruct((nc, n_qk, q_len, GDN_DK), f32),
                   jax.ShapeDtypeStruct((nc, n_v, q_len, GDN_DV), f32),
                   jax.ShapeDtypeStruct((nc, n_v, q_len), f32),
                   jax.ShapeDtypeStruct((nc, n_v, q_len), f32)),
        grid=grid,
        in_specs=[cur_spec, halo_spec, prev_spec, _full_spec((CONV_W, cdim)), small, small,
                  _full_spec((1, n_v)), _full_spec((1, n_v))],
        out_specs=(_per_chunk_spec(n_chunk, (n_qk, q_len, GDN_DK)), _per_chunk_spec(n_chunk, (n_qk, q_len, GDN_DK)),
                   _per_chunk_spec(n_chunk, (n_v, q_len, GDN_DV)), _per_chunk_spec(n_chunk, (n_v, q_len)),
                   _per_chunk_spec(n_chunk, (n_v, q_len))),
        scratch_shapes=[pltpu.VMEM((SUBLANES + q_len, cdim), f32)],
        compiler_params=_cparams("parallel", "arbitrary"),
    )(qkv, qkv, prev8, p['gdn_conv_w'], b_raw, a_raw, p['gdn_a_log'].reshape(1, n_v),
      p['gdn_dt_bias'].reshape(1, n_v))

    vdim = n_v * GDN_DV
    if q_len * GDN_PACK == MXU_COLS:
        w = GDN_PACK * q_len
        n_grp = n_v // GDN_PACK
        gt4 = gt.reshape(nc, n_grp, w)
        gx = jnp.repeat(jnp.swapaxes(gt, 1, 2), q_len, axis=2)
        bx = jnp.repeat(jnp.swapaxes(bt, 1, 2), q_len, axis=2)
        state_spec = pl.BlockSpec((1, n_v, GDN_DK, GDN_DV), lambda s, c: (s, 0, 0, 0), pipeline_mode=pl.Buffered(1))
        return pl.pallas_call(
            functools.partial(_gdn_chunk_packed_kernel, q_len=q_len, n_v=n_v, rep=n_v // n_qk, n_chunk=n_chunk),
            out_shape=(jax.ShapeDtypeStruct((nc * q_len, vdim), jnp.bfloat16),
                       jax.ShapeDtypeStruct((n_seq, n_v, GDN_DK, GDN_DV), f32)),
            grid=grid,
            in_specs=[_per_chunk_spec(n_chunk, (n_qk, q_len, GDN_DK)), _per_chunk_spec(n_chunk, (n_qk, q_len, GDN_DK)),
                      _per_chunk_spec(n_chunk, (n_v, q_len, GDN_DV)), _per_chunk_spec(n_chunk, (n_grp, w)),
                      _per_chunk_spec(n_chunk, (q_len, n_v * q_len)), _per_chunk_spec(n_chunk, (q_len, n_v * q_len)),
                      _row_chunk_spec(rb0, n_chunk, q_len, vdim), _full_spec((1, GDN_DV)), state_spec],
            out_specs=(_row_chunk_spec(0, n_chunk, q_len, vdim), state_spec),
            scratch_shapes=[pltpu.VMEM((n_grp, GDN_PACK * GDN_DK, GDN_PACK * GDN_DV), f32)],
            compiler_params=_cparams("parallel", "arbitrary"),
        )(qh, kh, vh, gt4, gx, bx, z, p['gdn_norm'].reshape(1, GDN_DV), s0)
    o, s_fin = pl.pallas_call(
        functools.partial(_gdn_chunk_kernel, q_len=q_len, n_qk=n_qk, rep=n_v // n_qk),
        out_shape=(jax.ShapeDtypeStruct((nc * q_len, vdim), jnp.bfloat16),
                   jax.ShapeDtypeStruct((n_seq, n_v, GDN_DK, GDN_DV), f32)),
        grid=grid,
        in_specs=[_per_chunk_spec(n_chunk, (n_qk, q_len, GDN_DK)), _per_chunk_spec(n_chunk, (n_qk, q_len, GDN_DK)),
                  _per_chunk_spec(n_chunk, (n_v, q_len, GDN_DV)), _per_chunk_spec(n_chunk, (n_v, q_len)),
                  _per_chunk_spec(n_chunk, (n_v, q_len)), _row_chunk_spec(rb0, n_chunk, q_len, vdim),
                  _full_spec((1, GDN_DV)), _per_seq_spec((n_v, GDN_DK, GDN_DV))],
        out_specs=(_row_chunk_spec(0, n_chunk, q_len, vdim), _per_seq_spec((n_v, GDN_DK, GDN_DV))),
        scratch_shapes=[pltpu.VMEM((n_v, q_len, GDN_DV), f32)],
        compiler_params=_cparams("parallel", "arbitrary"),
    )(qh, kh, vh, gt, bt, z, p['gdn_norm'].reshape(1, GDN_DV), s0)
    return o, s_fin


def _ssd_prep_kernel(cur_ref, halo_ref, prev_ref, w_ref, b_ref, dtraw_ref, dtb_ref, alog_ref,
                     xh_ref, bh_ref, ch_ref, cum_ref, dt_ref, cumt_ref, dtt_ref, xp_ref,
                     *, q_len, n_pair, n_grp):
    _conv_rows_into(xp_ref, cur_ref, halo_ref, prev_ref, pl.program_id(1) == 0, q_len)

    def tile(t):
        lanes = slice(t * LANES, (t + 1) * LANES)
        return _silu(_conv_tile(xp_ref, w_ref, lanes, q_len) + b_ref[0:1, lanes])

    for t in range(n_pair):
        xh_ref[0, t] = tile(t)
    for t in range(n_grp):
        bh_ref[0, t] = tile(n_pair + t)
    for t in range(n_grp):
        ch_ref[0, t] = tile(n_pair + n_grp + t)

    dt = _softplus(dtraw_ref[...] + dtb_ref[...])
    dta = dt * (-jnp.exp(alog_ref[...]))
    ri, ci = _iota2((q_len, q_len))
    cum_ref[0] = _dot((ri >= ci).astype(jnp.float32), dta)
    dt_ref[0] = dt
    r2, c2 = _iota2((q_len, 2 * q_len))
    c2 = jnp.where(c2 >= q_len, c2 - q_len, c2)
    cumt_ref[0] = _dot_tn(dta, (r2 <= c2).astype(jnp.float32))
    dtt_ref[0] = _dot_tn(dt, (r2 == c2).astype(jnp.float32))


def _ssd_chunk_kernel(xh_ref, bh_ref, ch_ref, cum_ref, dt_ref, cumt_ref, dtt_ref, z_ref, d_ref, nw_ref, s0_ref,
                      o_ref, s_ref, y_scr, *, q_len, n_pair, n_grp):
    @pl.when(pl.program_id(1) == 0)
    def _():
        s_ref[...] = s0_ref[...]

    n_head = 2 * n_pair
    pairs_per_grp = n_pair // n_grp
    half = LANES // 2
    r2, c2 = _iota2((q_len, 2 * q_len))
    second2 = c2 >= q_len
    causal2 = r2 >= jnp.where(second2, c2 - q_len, c2)
    lane_row2 = lax.broadcasted_iota(jnp.int32, (1, 2 * q_len), 1) >= q_len
    hx, lx = _iota2((n_head, LANES))
    h2, l2 = _iota2((n_head, 2 * q_len))
    rx, cx = _iota2((LANES, LANES))
    eye_x = rx == cx
    lane_x = lax.broadcasted_iota(jnp.int32, (1, LANES), 1)
    cum = cum_ref[0]
    dt = dt_ref[0]

    def pair(p, carry):
        g = p // pairs_per_grp
        h0 = 2 * p
        x = xh_ref[0, p]
        bg = bh_ref[0, g]
        cg = ch_ref[0, g]
        cb2 = _bdot_nt(cg, jnp.concatenate([bg, bg], axis=0))
        crow2 = jnp.where(lane_row2, cumt_ref[0, pl.ds(h0 + 1, 1), :], cumt_ref[0, pl.ds(h0, 1), :])
        dtrow2 = jnp.where(lane_row2, dtt_ref[0, pl.ds(h0 + 1, 1), :], dtt_ref[0, pl.ds(h0, 1), :])
        sel2 = (h2 == h0 + (l2 >= q_len).astype(jnp.int32)).astype(jnp.float32)
        selx = (hx == h0 + (lx >= half).astype(jnp.int32)).astype(jnp.float32)
        ccol2 = _dot(cum, sel2)
        ccolx = _dot(cum, selx)
        dtcolx = _dot(dt, selx)
        m = jnp.where(causal2, cb2 * jnp.exp(jnp.where(causal2, ccol2 - crow2, 0.0)) * dtrow2, 0.0)
        xstack = jnp.concatenate([jnp.where(lane_x < half, x, 0.0), jnp.where(lane_x >= half, x, 0.0)], axis=0)
        s = s_ref[0, p]
        y = _bdot(m, xstack) + _bdot_nt(cg, s) * jnp.exp(ccolx) + d_ref[0, pl.ds(p, 1), :] * x
        clastx = ccolx[q_len - 1:q_len, :]
        wx = jnp.exp(clastx - ccolx) * dtcolx
        dec_col = jnp.sum(jnp.where(eye_x, jnp.exp(clastx), 0.0), axis=1, keepdims=True)
        s_ref[0, p] = s * dec_col + _bdot_tn(x * wx, bg)
        y_scr[p] = y
        return carry

    lax.fori_loop(0, n_pair, pair, 0)

    gsz = pairs_per_grp * LANES
    for g in range(n_grp):
        parts = []
        ss = None
        for i in range(pairs_per_grp):
            t = g * pairs_per_grp + i
            lanes = slice(t * LANES, (t + 1) * LANES)
            yz = y_scr[t] * _silu(z_ref[:, lanes])
            parts.append((lanes, yz))
            sq = jnp.sum(yz * yz, axis=-1, keepdims=True)
            ss = sq if ss is None else ss + sq
        scale = lax.rsqrt(ss / gsz + NORM_EPS)
        for lanes, yz in parts:
            o_ref[:, lanes] = ((yz * scale) * nw_ref[0:1, lanes]).astype(o_ref.dtype)


def ssd_group(xbc, z, dt_raw, prev8, s0, p, *, row0, n_seq, n_chunk, q_len):
    n_pair = MB_HEADS // 2
    n_grp = MB_GROUPS
    n_head = MB_HEADS
    cdim = xbc.shape[1]
    nc = n_seq * n_chunk
    rb0 = row0 // q_len
    grid = (n_seq, n_chunk)
    f32 = jnp.float32
    cur_spec, halo_spec, prev_spec = _chunk_specs(rb0, n_chunk, q_len, cdim)
    xh, bh, ch, cum, dt, cumt, dtt = pl.pallas_call(
        functools.partial(_ssd_prep_kernel, q_len=q_len, n_pair=n_pair, n_grp=n_grp),
        out_shape=(jax.ShapeDtypeStruct((nc, n_pair, q_len, LANES), f32),
                   jax.ShapeDtypeStruct((nc, n_grp, q_len, MB_STATE), f32),
                   jax.ShapeDtypeStruct((nc, n_grp, q_len, MB_STATE), f32),
                   jax.ShapeDtypeStruct((nc, q_len, n_head), f32),
                   jax.ShapeDtypeStruct((nc, q_len, n_head), f32),
                   jax.ShapeDtypeStruct((nc, n_head, 2 * q_len), f32),
                   jax.ShapeDtypeStruct((nc, n_head, 2 * q_len), f32)),
        grid=grid,
        in_specs=[cur_spec, halo_spec, prev_spec, _full_spec((CONV_W, cdim)), _full_spec((1, cdim)),
                  _row_chunk_spec(rb0, n_chunk, q_len, n_head), _full_spec((1, n_head)), _full_spec((1, n_head))],
        out_specs=(_per_chunk_spec(n_chunk, (n_pair, q_len, LANES)), _per_chunk_spec(n_chunk, (n_grp, q_len, MB_STATE)),
                   _per_chunk_spec(n_chunk, (n_grp, q_len, MB_STATE)), _per_chunk_spec(n_chunk, (q_len, n_head)),
                   _per_chunk_spec(n_chunk, (q_len, n_head)), _per_chunk_spec(n_chunk, (n_head, 2 * q_len)),
                   _per_chunk_spec(n_chunk, (n_head, 2 * q_len))),
        scratch_shapes=[pltpu.VMEM((SUBLANES + q_len, cdim), f32)],
        compiler_params=_cparams("parallel", "arbitrary"),
    )(xbc, xbc, prev8, p['mb_conv_w'], p['mb_conv_b'].reshape(1, cdim), dt_raw,
      p['mb_dt_bias'].reshape(1, n_head), p['mb_a_log'].reshape(1, n_head))

    d_lanes = jnp.repeat(p['mb_d'], MB_HEADDIM).reshape(1, n_pair, LANES)
    y, s_fin = pl.pallas_call(
        functools.partial(_ssd_chunk_kernel, q_len=q_len, n_pair=n_pair, n_grp=n_grp),
        out_shape=(jax.ShapeDtypeStruct((nc * q_len, MB_DIM), jnp.bfloat16),
                   jax.ShapeDtypeStruct((n_seq, n_pair, LANES, MB_STATE), f32)),
        grid=grid,
        in_specs=[_per_chunk_spec(n_chunk, (n_pair, q_len, LANES)), _per_chunk_spec(n_chunk, (n_grp, q_len, MB_STATE)),
                  _per_chunk_spec(n_chunk, (n_grp, q_len, MB_STATE)), _per_chunk_spec(n_chunk, (q_len, n_head)),
                  _per_chunk_spec(n_chunk, (q_len, n_head)), _per_chunk_spec(n_chunk, (n_head, 2 * q_len)),
                  _per_chunk_spec(n_chunk, (n_head, 2 * q_len)), _row_chunk_spec(rb0, n_chunk, q_len, MB_DIM),
                  _full_spec((1, n_pair, LANES)), _full_spec((1, MB_DIM)), _per_seq_spec((n_pair, LANES, MB_STATE))],
        out_specs=(_row_chunk_spec(0, n_chunk, q_len, MB_DIM), _per_seq_spec((n_pair, LANES, MB_STATE))),
        scratch_shapes=[pltpu.VMEM((n_pair, q_len, LANES), f32)],
        compiler_params=_cparams("parallel", "arbitrary"),
    )(xh, bh, ch, cum, dt, cumt, dtt, z, d_lanes, p['mb_norm'].reshape(1, MB_DIM), s0)
    return y, s_fin


def _s5_step(h_re, h_im, a_re, a_im, bu_re, bu_im):
    return a_re * h_re - a_im * h_im + bu_re, a_re * h_im + a_im * h_re + bu_im


def _s5_pass1_kernel(u_ref, bre_ref, bim_ref, are_ref, aim_ref, vre_ref, vim_ref):
    a_re, a_im = are_ref[...], aim_ref[...]
    b_re, b_im = bre_ref[0].astype(jnp.bfloat16), bim_ref[0].astype(jnp.bfloat16)
    h_re = _bdot(u_ref[0], b_re)
    h_im = _bdot(u_ref[0], b_im)
    for t in range(1, S5_SUB):
        u = u_ref[t].astype(jnp.bfloat16)
        h_re, h_im = _s5_step(h_re, h_im, a_re, a_im, _bdot(u, b_re), _bdot(u, b_im))
    vre_ref[...] = h_re
    vim_ref[...] = h_im


def _s5_pass2_kernel(u_ref, hre_ref, him_ref, bre_ref, bim_ref, cre_ref, cim_ref, are_ref, aim_ref, d_ref, y_ref):
    a_re, a_im = are_ref[...], aim_ref[...]
    b_re, b_im = bre_ref[0].astype(jnp.bfloat16), bim_ref[0].astype(jnp.bfloat16)
    c_re, c_im = cre_ref[0].astype(jnp.bfloat16), cim_ref[0].astype(jnp.bfloat16)
    h_re, h_im = hre_ref[...], him_ref[...]
    for t in range(S5_SUB):
        u = u_ref[t]
        u16 = u.astype(jnp.bfloat16)
        h_re, h_im = _s5_step(h_re, h_im, a_re, a_im, _bdot(u16, b_re), _bdot(u16, b_im))
        y_ref[t] = _bdot(h_re, c_re) - _bdot(h_im, c_im) + d_ref[...] * u


def _s5_carry_kernel(vre_ref, vim_ref, are_ref, aim_ref, s0re_ref, s0im_ref,
                     hre_ref, him_ref, pre_ref, pim_ref, sre_ref, sim_ref, *, n_main, n_dec):
    a_re, a_im = are_ref[...], aim_ref[...]
    half = SUBLANES // 2
    meta = pl.ds(n_main, SUBLANES)
    row = lax.broadcasted_iota(jnp.int32, (SUBLANES, vre_ref.shape[1]), 0)
    first_half = row < half

    def cmul_add(h_re, h_im, v_re, v_im):
        return a_re * h_re - a_im * h_im + v_re, a_re * h_im + a_im * h_re + v_im

    def two_pieces(i, carry):
        h_re, h_im = carry
        rows = pl.ds(pl.multiple_of(i * SUBLANES, SUBLANES), SUBLANES)
        v_re, v_im = vre_ref[rows, :], vim_ref[rows, :]
        t_re, t_im = cmul_add(h_re, h_im, v_re, v_im)
        t_re = pltpu.roll(t_re, half, 0)
        t_im = pltpu.roll(t_im, half, 0)
        hre_ref[rows, :] = jnp.where(first_half, h_re, t_re)
        him_ref[rows, :] = jnp.where(first_half, h_im, t_im)
        n_re, n_im = cmul_add(t_re, t_im, v_re, v_im)
        return pltpu.roll(n_re, half, 0), pltpu.roll(n_im, half, 0)

    h_re, h_im = lax.fori_loop(0, n_main // SUBLANES, two_pieces, (vre_ref[meta, :], vim_ref[meta, :]))
    pre_ref[...] = h_re
    pim_ref[...] = h_im
    hre_ref[meta, :] = jnp.zeros((SUBLANES, vre_ref.shape[1]), jnp.float32)
    him_ref[meta, :] = jnp.zeros((SUBLANES, vre_ref.shape[1]), jnp.float32)
    dec = pl.ds(n_main + SUBLANES, n_dec)
    s0_re, s0_im = s0re_ref[...], s0im_ref[...]
    hre_ref[dec, :] = s0_re
    him_ref[dec, :] = s0_im
    f_re, f_im = cmul_add(s0_re, s0_im, vre_ref[dec, :], vim_ref[dec, :])
    sre_ref[...] = f_re
    sim_ref[...] = f_im
    tail = pl.ds(n_main + SUBLANES + n_dec, SUBLANES)
    hre_ref[tail, :] = jnp.zeros((SUBLANES, vre_ref.shape[1]), jnp.float32)
    him_ref[tail, :] = jnp.zeros((SUBLANES, vre_ref.shape[1]), jnp.float32)


def _s5_post_kernel(y_ref, w_ref, b_ref, nw_ref, o_ref, wbf_ref):
    @pl.when(pl.program_id(0) == 0)
    def _():
        wbf_ref[...] = w_ref[...].astype(jnp.bfloat16)

    gl = jax.nn.gelu(y_ref[...])
    gate = jnp.dot(gl.astype(jnp.bfloat16), wbf_ref[...], preferred_element_type=jnp.float32) + b_ref[...]
    o_ref[...] = _rms(gl * jax.nn.sigmoid(gate), nw_ref[...]).astype(o_ref.dtype)


def _s5_discretize(p):
    lam_re, lam_im = p['s5_lambda_re'], p['s5_lambda_im']
    step = jnp.exp(p['s5_log_step'])[:, None]
    mag = jnp.exp(lam_re * step)
    a_re = mag * jnp.cos(lam_im * step)
    a_im = mag * jnp.sin(lam_im * step)
    den = lam_re * lam_re + lam_im * lam_im
    f_re = ((a_re - 1.0) * lam_re + a_im * lam_im) / den
    f_im = (a_im * lam_re - (a_re - 1.0) * lam_im) / den
    bb_re = f_re[..., None] * p['s5_b_re'] - f_im[..., None] * p['s5_b_im']
    bb_im = f_re[..., None] * p['s5_b_im'] + f_im[..., None] * p['s5_b_re']
    return a_re, a_im, bb_re, bb_im


def _block_diag(w):
    g, r, c = w.shape
    nb = g // S5_GB
    eye = jnp.eye(S5_GB, dtype=w.dtype)
    return jnp.einsum('bgrc,gh->bgrhc', w.reshape(nb, S5_GB, r, c), eye).reshape(nb, S5_GB * r, S5_GB * c)


def s5_mixer_rows(u, s0_re, s0_im, p, *, n_prompt, seq, n_dec):
    f32 = jnp.float32
    n_main = n_prompt * seq
    n_meta = n_prompt * N_META
    assert n_prompt == SUBLANES // 2 and seq % (2 * S5_SUB) == 0 and n_dec % SUBLANES == 0
    pieces = seq // S5_SUB
    n_main_p = pieces * n_prompt
    n_piece = n_main_p + SUBLANES + n_dec + SUBLANES
    assert n_piece % S5_ROWS == 0
    a_re, a_im, bb_re, bb_im = _s5_discretize(p)
    sdim = S5_GROUPS * S5_STATE
    nb = S5_GROUPS // S5_GB
    gw = S5_GB * S5_STATE
    uw = S5_GB * S5_GROUP
    b_re = _block_diag(jnp.swapaxes(bb_re, 1, 2))
    b_im = _block_diag(jnp.swapaxes(bb_im, 1, 2))
    c_re = _block_diag(jnp.swapaxes(p['s5_c_re'], 1, 2))
    c_im = _block_diag(jnp.swapaxes(p['s5_c_im'], 1, 2))
    a16_re, a16_im = a_re, a_im
    for _ in range(int(math.log2(S5_SUB))):
        a16_re, a16_im = a16_re * a16_re - a16_im * a16_im, 2.0 * a16_re * a16_im
    flat = lambda a: a.reshape(1, sdim)

    um = u[:n_main].reshape(n_prompt, pieces, S5_SUB, S5_DIM).transpose(2, 1, 0, 3).reshape(S5_SUB, n_main_p, S5_DIM)
    ue = u[n_main:n_main + n_meta].reshape(n_prompt, S5_SUB, S5_DIM).transpose(1, 0, 2)
    us = u[n_main + n_meta:].reshape(n_dec, S5_SUB, S5_DIM).transpose(1, 0, 2)
    zpad = jnp.zeros((S5_SUB, SUBLANES - n_prompt, S5_DIM), f32)
    u4 = jnp.concatenate([um, ue, zpad, us, jnp.zeros((S5_SUB, SUBLANES, S5_DIM), f32)], axis=1)

    grid = (nb, n_piece // S5_ROWS)
    u_spec = pl.BlockSpec((S5_SUB, S5_ROWS, uw), lambda b, r: (0, r, b))
    st_spec = pl.BlockSpec((S5_ROWS, gw), lambda b, r: (r, b))
    bw_spec = pl.BlockSpec((1, uw, gw), lambda b, r: (b, 0, 0))
    cw_spec = pl.BlockSpec((1, gw, uw), lambda b, r: (b, 0, 0))
    a_spec = pl.BlockSpec((1, gw), lambda b, r: (0, b))
    v_re, v_im = pl.pallas_call(
        _s5_pass1_kernel,
        out_shape=(jax.ShapeDtypeStruct((n_piece, sdim), f32), jax.ShapeDtypeStruct((n_piece, sdim), f32)),
        grid=grid, in_specs=[u_spec, bw_spec, bw_spec, a_spec, a_spec], out_specs=(st_spec, st_spec),
        compiler_params=_cparams("parallel", "parallel"),
    )(u4, b_re, b_im, flat(a_re), flat(a_im))

    lanes_blk = 1024
    col = lambda rows: pl.BlockSpec((rows, lanes_blk), lambda j: (0, j))
    h_re, h_im, p_re, p_im, s_re, s_im = pl.pallas_call(
        functools.partial(_s5_carry_kernel, n_main=n_main_p, n_dec=n_dec),
        out_shape=(jax.ShapeDtypeStruct((n_piece, sdim), f32), jax.ShapeDtypeStruct((n_piece, sdim), f32),
                   jax.ShapeDtypeStruct((SUBLANES, sdim), f32), jax.ShapeDtypeStruct((SUBLANES, sdim), f32),
                   jax.ShapeDtypeStruct((n_dec, sdim), f32), jax.ShapeDtypeStruct((n_dec, sdim), f32)),
        grid=(sdim // lanes_blk,),
        in_specs=[col(n_piece), col(n_piece), col(1), col(1), col(n_dec), col(n_dec)],
        out_specs=(col(n_piece), col(n_piece), col(SUBLANES), col(SUBLANES), col(n_dec), col(n_dec)),
        compiler_params=_cparams("parallel"),
    )(v_re, v_im, flat(a16_re), flat(a16_im), s0_re, s0_im)

    y4 = pl.pallas_call(
        _s5_pass2_kernel,
        out_shape=jax.ShapeDtypeStruct((S5_SUB, n_piece, S5_DIM), f32),
        grid=grid,
        in_specs=[u_spec, st_spec, st_spec, bw_spec, bw_spec, cw_spec, cw_spec, a_spec, a_spec,
                  pl.BlockSpec((1, uw), lambda b, r: (0, b))],
        out_specs=u_spec,
        compiler_params=_cparams("parallel", "parallel"),
    )(u4, h_re, h_im, b_re, b_im, c_re, c_im, flat(a_re), flat(a_im), p['s5_d'].reshape(1, S5_DIM))

    ym = y4[:, :n_main_p].reshape(S5_SUB, pieces, n_prompt, S5_DIM).transpose(2, 1, 0, 3).reshape(n_main, S5_DIM)
    ye = y4[:, n_main_p:n_main_p + n_prompt].transpose(1, 0, 2).reshape(n_meta, S5_DIM)
    ys = y4[:, n_main_p + SUBLANES:n_main_p + SUBLANES + n_dec].transpose(1, 0, 2).reshape(n_dec * S5_SUB, S5_DIM)
    y = jnp.concatenate([ym, ye, ys], axis=0)

    m = y.shape[0]
    mixed = pl.pallas_call(
        _s5_post_kernel,
        out_shape=jax.ShapeDtypeStruct((m, S5_DIM), jnp.bfloat16),
        grid=(m // ROW_TILE,),
        in_specs=[_row_spec(S5_DIM),
                  pl.BlockSpec((S5_DIM, S5_DIM), lambda i: (0, 0), pipeline_mode=pl.Buffered(1)),
                  _full_spec((1, S5_DIM)), _full_spec((1, S5_DIM))],
        out_specs=_row_spec(S5_DIM),
        scratch_shapes=[pltpu.VMEM((S5_DIM, S5_DIM), jnp.bfloat16)],
        compiler_params=_cparams("arbitrary"),
    )(y, p['s5_glu_w'], p['s5_glu_b'].reshape(1, S5_DIM), p['s5_norm'].reshape(1, S5_DIM))
    return mixed, p_re[:n_prompt], p_im[:n_prompt], s_re, s_im


def _conv_history(x, cache, n_prompt, seq, n_dec, dec_len):
    n_main = n_prompt * seq
    n_meta = n_prompt * N_META
    k = CONV_W - 1
    cdim = x.shape[1]
    pad = ((0, 0), (SUBLANES - k, 0), (0, 0))
    meta = x[n_main:n_main + n_meta].reshape(n_prompt, N_META, cdim)
    prev_short = jnp.concatenate([jnp.zeros((n_prompt, SUBLANES, cdim), x.dtype), jnp.pad(cache, pad)], axis=0)
    prev_main = jnp.pad(meta[:, N_META - k:], pad)
    new_prompt = x[:n_main].reshape(n_prompt, seq, cdim)[:, seq - k:]
    new_sample = x[n_main + n_meta:].reshape(n_dec, dec_len, cdim)[:, dec_len - k:]
    return prev_short, prev_main, new_prompt, new_sample


def _two_groups(group_fn, prev_short, prev_main, s0_sample, n_prompt, seq, n_dec, dec_len):
    n_main = n_prompt * seq
    s0_short = jnp.concatenate([jnp.zeros((n_prompt,) + s0_sample.shape[1:], s0_sample.dtype), s0_sample], axis=0)
    y_short, s_short = group_fn(prev_short, s0_short, row0=n_main, n_seq=n_prompt + n_dec, n_chunk=1, q_len=N_META)
    y_main, s_main = group_fn(prev_main, s_short[:n_prompt], row0=0, n_seq=n_prompt, n_chunk=seq // CHUNK,
                              q_len=CHUNK)
    return jnp.concatenate([y_main, y_short], axis=0), s_main, s_short[n_prompt:]


def kernel(x_prompt, x_sample, state_s5_re, state_s5_im, state_ssd, cache_ssd_conv, state_gdn, cache_gdn_conv, meta_tokens, norm_mix0, w_in0, s5_lambda_re, s5_lambda_im, s5_log_step, s5_b_re, s5_b_im, s5_c_re, s5_c_im, s5_d, s5_glu_w, s5_glu_b, s5_norm, mb_conv_w, mb_conv_b, mb_dt_bias, mb_a_log, mb_d, mb_norm, w_out0, norm_ffn0, ffn_w_gate, ffn_w_up, ffn_w_down, norm_mix1, w_in1, gdn_conv_w, gdn_a_log, gdn_dt_bias, gdn_norm, w_out1, norm_ffn1, moe_router, moe_w_gate, moe_w_up, moe_w_down, norm_final):
    p = dict(s5_lambda_re=s5_lambda_re, s5_lambda_im=s5_lambda_im, s5_log_step=s5_log_step, s5_b_re=s5_b_re,
             s5_b_im=s5_b_im, s5_c_re=s5_c_re, s5_c_im=s5_c_im, s5_d=s5_d, s5_glu_w=s5_glu_w, s5_glu_b=s5_glu_b,
             s5_norm=s5_norm, mb_conv_w=mb_conv_w, mb_conv_b=mb_conv_b, mb_dt_bias=mb_dt_bias, mb_a_log=mb_a_log,
             mb_d=mb_d, mb_norm=mb_norm, gdn_conv_w=gdn_conv_w, gdn_a_log=gdn_a_log, gdn_dt_bias=gdn_dt_bias,
             gdn_norm=gdn_norm)
    n_prompt, seq, d = x_prompt.shape
    n_dec, dec_len, _ = x_sample.shape
    assert dec_len == N_META and seq % CHUNK == 0
    n_main = n_prompt * seq
    n_meta = n_prompt * N_META
    groups = (n_prompt, seq, n_dec, dec_len)

    x0 = jnp.concatenate([x_prompt.reshape(n_main, d), jnp.tile(meta_tokens, (n_prompt, 1)),
                          x_sample.reshape(n_dec * dec_len, d)], axis=0)

    h = rms_norm_rows(x0, norm_mix0, jnp.bfloat16)
    o1 = S5_DIM
    o2 = o1 + MB_DIM
    o3 = o2 + MB_CONV_DIM
    u = matmul_rows(h, w_in0, col0=0, ncols=S5_DIM)
    z0 = matmul_rows(h, w_in0, col0=o1, ncols=MB_DIM)
    xbc = matmul_rows(h, w_in0, col0=o2, ncols=MB_CONV_DIM)
    dt_raw = matmul_rows(h, w_in0[:, o3:])

    y_s5, p_s5_re, p_s5_im, s_s5_re, s_s5_im = s5_mixer_rows(
        u, state_s5_re.reshape(n_dec, -1), state_s5_im.reshape(n_dec, -1), p, n_prompt=n_prompt, seq=seq, n_dec=n_dec)
    s5_shape = (S5_GROUPS, S5_STATE)

    prev_short, prev_main, p_ssd_conv, s_ssd_conv = _conv_history(xbc, cache_ssd_conv, *groups)
    ssd_fn = functools.partial(ssd_group, xbc, z0, dt_raw)
    y_mb, p_ssd, s_ssd = _two_groups(
        lambda prev8, s0, **kw: ssd_fn(prev8, s0, p, **kw), prev_short, prev_main,
        state_ssd.reshape(n_dec, MB_HEADS // 2, LANES, MB_STATE), *groups)
    ssd_shape = (MB_HEADS, MB_HEADDIM, MB_STATE)

    mixed = jnp.concatenate([y_s5, y_mb], axis=1)
    x1 = matmul_rows(mixed, w_out0, tn=256, residual=x0)
    h = rms_norm_rows(x1, norm_ffn0, jnp.bfloat16)
    ffn = dense_swiglu(h, ffn_w_gate, ffn_w_up, ffn_w_down)

    x2, h = add_rms_norm_rows(x1, ffn, norm_mix1)
    o1 = GDN_CONV_DIM
    o2 = o1 + GDN_V_DIM
    o3 = o2 + GDN_V_HEADS
    qkv = matmul_rows(h, w_in1, col0=0, ncols=GDN_CONV_DIM)
    z1 = matmul_rows(h, w_in1, col0=o1, ncols=GDN_V_DIM)
    b_raw = matmul_rows(h, w_in1[:, o2:o3])
    a_raw = matmul_rows(h, w_in1[:, o3:])
    prev_short, prev_main, p_gdn_conv, s_gdn_conv = _conv_history(qkv, cache_gdn_conv, *groups)
    gdn_fn = functools.partial(gdn_group, qkv, z1, b_raw, a_raw)
    y_gdn, p_gdn, s_gdn = _two_groups(
        lambda prev8, s0, **kw: gdn_fn(prev8, s0, p, **kw), prev_short, prev_main, state_gdn, *groups)

    x3 = matmul_rows(y_gdn, w_out1, tn=256, residual=x2)
    h, logits = rms_norm_router_rows(x3, norm_ffn1, moe_router)
    moe = moe_swiglu_rows(h, logits, moe_w_gate, moe_w_up, moe_w_down)
    y = add_rms_norm_only_rows(x3, moe, norm_final, jnp.float32)

    y_prompt = y[:n_main].reshape(n_prompt, seq, d)
    y_sample = y[n_main + n_meta:].reshape(n_dec, dec_len, d)
    return (y_prompt, y_sample,
            p_s5_re.reshape((n_prompt,) + s5_shape), p_s5_im.reshape((n_prompt,) + s5_shape),
            p_ssd.reshape((n_prompt,) + ssd_shape), p_ssd_conv, p_gdn, p_gdn_conv,
            s_s5_re.reshape((n_dec,) + s5_shape), s_s5_im.reshape((n_dec,) + s5_shape),
            s_ssd.reshape((n_dec,) + ssd_shape), s_ssd_conv, s_gdn, s_gdn_conv)
```

```python
import functools
import math

import jax
import jax.numpy as jnp
from jax import lax
from jax.experimental import pallas as pl
from jax.experimental.pallas import tpu as pltpu

D_MODEL = 4096
CHUNK = 64
N_META = 16
NORM_EPS = 1e-6
CONV_W = 4

S5_GROUP = 16
S5_DIM = D_MODEL // 2
S5_GROUPS = S5_DIM // S5_GROUP
S5_STATE = 64
MB_HEADDIM = 64
MB_DIM = D_MODEL
MB_HEADS = MB_DIM // MB_HEADDIM
MB_GROUPS = 8
MB_STATE = 128
MB_CONV_DIM = MB_DIM + 2 * MB_GROUPS * MB_STATE

GDN_QK_HEADS = 32
GDN_V_HEADS = 64
GDN_DK = 128
GDN_DV = 128
GDN_QK_DIM = GDN_QK_HEADS * GDN_DK
GDN_V_DIM = GDN_V_HEADS * GDN_DV
GDN_CONV_DIM = 2 * GDN_QK_DIM + GDN_V_DIM

N_EXPERTS = 8
TOP_K = 2

V7X_VMEM_LIMIT_BYTES = 56 * 1024 * 1024
MXU_COLS = 256
LANES = 128
SUBLANES = 8
ROW_TILE = 224
MM_ROWS = 1216
FFN_ROWS = 1024
FFN_SUB = 256
FFN_TF = MXU_COLS
GDN_PACK = 4
GDN_INTERLEAVE = 4
SSD_INTERLEAVE = 4
S5_SUB = 16
S5_GB = MXU_COLS // S5_GROUP
S5_ROWS = 272

_HI = lax.Precision.HIGHEST


def _cparams(*sem):
    return pltpu.CompilerParams(dimension_semantics=sem, vmem_limit_bytes=V7X_VMEM_LIMIT_BYTES)


def _dot(a, b):
    return jnp.dot(a, b, precision=_HI, preferred_element_type=jnp.float32)


def _dot_nt(a, b):
    return lax.dot_general(a, b, (((1,), (1,)), ((), ())), precision=_HI, preferred_element_type=jnp.float32)


def _dot_tn(a, b):
    return lax.dot_general(a, b, (((0,), (0,)), ((), ())), precision=_HI, preferred_element_type=jnp.float32)


def _bdot(a, b):
    return jnp.dot(a.astype(jnp.bfloat16), b.astype(jnp.bfloat16), preferred_element_type=jnp.float32)


def _bdot_nt(a, b):
    return lax.dot_general(a.astype(jnp.bfloat16), b.astype(jnp.bfloat16), (((1,), (1,)), ((), ())),
                           preferred_element_type=jnp.float32)


def _bdot_tn(a, b):
    return lax.dot_general(a.astype(jnp.bfloat16), b.astype(jnp.bfloat16), (((0,), (0,)), ((), ())),
                           preferred_element_type=jnp.float32)


def _silu(x):
    return x * jax.nn.sigmoid(x)


def _softplus(x):
    return jnp.maximum(x, 0.0) + jnp.log1p(jnp.exp(-jnp.abs(x)))


def _rms(x, w):
    ms = jnp.mean(x * x, axis=-1, keepdims=True)
    return (x * lax.rsqrt(ms + NORM_EPS)) * w


def _norm_kernel(x_ref, w_ref, h_ref):
    h_ref[...] = _rms(x_ref[...], w_ref[...]).astype(h_ref.dtype)


def _add_norm_kernel(x_ref, d_ref, w_ref, xo_ref, h_ref):
    x = x_ref[...] + d_ref[...]
    xo_ref[...] = x
    h_ref[...] = _rms(x, w_ref[...]).astype(h_ref.dtype)


def _add_norm_only_kernel(x_ref, d_ref, w_ref, h_ref):
    h_ref[...] = _rms(x_ref[...] + d_ref[...], w_ref[...]).astype(h_ref.dtype)


def _norm_router_kernel(x_ref, w_ref, wr_ref, h_ref, lg_ref):
    h = _rms(x_ref[...], w_ref[...])
    h_ref[...] = h.astype(h_ref.dtype)
    lg_ref[...] = _dot(h, wr_ref[...])


def _row_spec(cols):
    return pl.BlockSpec((ROW_TILE, cols), lambda i: (i, 0))


def _full_spec(shape):
    return pl.BlockSpec(shape, lambda *_: (0,) * len(shape))


def rms_norm_rows(x, w, out_dtype):
    m, d = x.shape
    return pl.pallas_call(
        _norm_kernel,
        out_shape=jax.ShapeDtypeStruct((m, d), out_dtype),
        grid=(m // ROW_TILE,),
        in_specs=[_row_spec(d), _full_spec((1, d))],
        out_specs=_row_spec(d),
        compiler_params=_cparams("parallel"),
    )(x, w.reshape(1, d))


def add_rms_norm_rows(x, delta, w):
    m, d = x.shape
    return pl.pallas_call(
        _add_norm_kernel,
        out_shape=(jax.ShapeDtypeStruct((m, d), jnp.float32), jax.ShapeDtypeStruct((m, d), jnp.bfloat16)),
        grid=(m // ROW_TILE,),
        in_specs=[_row_spec(d), _row_spec(d), _full_spec((1, d))],
        out_specs=(_row_spec(d), _row_spec(d)),
        compiler_params=_cparams("parallel"),
    )(x, delta, w.reshape(1, d))


def add_rms_norm_only_rows(x, delta, w, out_dtype):
    m, d = x.shape
    return pl.pallas_call(
        _add_norm_only_kernel,
        out_shape=jax.ShapeDtypeStruct((m, d), out_dtype),
        grid=(m // ROW_TILE,),
        in_specs=[_row_spec(d), _row_spec(d), _full_spec((1, d))],
        out_specs=_row_spec(d),
        compiler_params=_cparams("parallel"),
    )(x, delta, w.reshape(1, d))


def rms_norm_router_rows(x, w, w_router):
    m, d = x.shape
    e = w_router.shape[1]
    return pl.pallas_call(
        _norm_router_kernel,
        out_shape=(jax.ShapeDtypeStruct((m, d), jnp.bfloat16), jax.ShapeDtypeStruct((m, e), jnp.float32)),
        grid=(m // ROW_TILE,),
        in_specs=[_row_spec(d), _full_spec((1, d)), _full_spec((d, e))],
        out_specs=(_row_spec(d), _row_spec(e)),
        compiler_params=_cparams("parallel"),
    )(x, w.reshape(1, d), w_router)


def _mm_kernel(x_ref, w_ref, o_ref):
    o_ref[...] = jnp.dot(x_ref[...], w_ref[...].astype(jnp.bfloat16),
                         preferred_element_type=jnp.float32).astype(o_ref.dtype)


def _mm_res_kernel(x_ref, w_ref, r_ref, o_ref):
    o_ref[...] = r_ref[...] + jnp.dot(x_ref[...], w_ref[...].astype(jnp.bfloat16),
                                      preferred_element_type=jnp.float32)


def matmul_rows(x, w, *, col0=0, ncols=None, tn=512, residual=None, out_dtype=jnp.float32):
    m, k = x.shape
    ncols = w.shape[1] - col0 if ncols is None else ncols
    if ncols < tn:
        tn = ncols
    assert m % MM_ROWS == 0 and ncols % tn == 0 and col0 % tn == 0
    cb = col0 // tn
    x_spec = pl.BlockSpec((MM_ROWS, k), lambda i, j: (i, 0), pipeline_mode=pl.Buffered(1))
    w_spec = pl.BlockSpec((k, tn), lambda i, j: (0, j + cb))
    o_spec = pl.BlockSpec((MM_ROWS, tn), lambda i, j: (i, j))
    grid = (m // MM_ROWS, ncols // tn)
    if residual is None:
        return pl.pallas_call(
            _mm_kernel, out_shape=jax.ShapeDtypeStruct((m, ncols), out_dtype), grid=grid,
            in_specs=[x_spec, w_spec], out_specs=o_spec,
            compiler_params=_cparams("parallel", "arbitrary"))(x, w)
    return pl.pallas_call(
        _mm_res_kernel, out_shape=jax.ShapeDtypeStruct((m, ncols), jnp.float32), grid=grid,
        in_specs=[x_spec, w_spec, o_spec], out_specs=o_spec,
        compiler_params=_cparams("parallel", "arbitrary"))(x, w, residual)


def _swiglu_kernel(ce_ref, cv_ref, x_ref, wg_ref, wu_ref, wd_ref, o_ref):
    c = pl.program_id(0)
    f = pl.program_id(1)
    n_valid = cv_ref[c]

    @pl.when(f == 0)
    def _():
        o_ref[...] = jnp.zeros_like(o_ref)

    @pl.when(n_valid > 0)
    def _():
        wg = wg_ref[...].astype(jnp.bfloat16)
        wu = wu_ref[...].astype(jnp.bfloat16)
        wd = wd_ref[...].astype(jnp.bfloat16)
        for s in range(FFN_ROWS // FFN_SUB):
            @pl.when(s < n_valid)
            def _():
                rows = pl.ds(s * FFN_SUB, FFN_SUB)
                xs = x_ref[rows, :]
                g = jnp.dot(xs, wg, preferred_element_type=jnp.float32)
                u = jnp.dot(xs, wu, preferred_element_type=jnp.float32)
                h = (_silu(g) * u).astype(jnp.bfloat16)
                o_ref[rows, :] += jnp.dot(h, wd, preferred_element_type=jnp.float32)


def swiglu_chunks(xs, wg, wu, wd, chunk_expert, chunk_valid):
    rows, d = xs.shape
    n_chunks = rows // FFN_ROWS
    ff = wg.shape[2]
    nf = ff // FFN_TF

    def f_idx(c, f, cv):
        return jnp.where(cv[c] > 0, f, nf - 1)

    grid_spec = pltpu.PrefetchScalarGridSpec(
        num_scalar_prefetch=2,
        grid=(n_chunks, nf),
        in_specs=[
            pl.BlockSpec((FFN_ROWS, d), lambda c, f, ce, cv: (c, 0), pipeline_mode=pl.Buffered(1)),
            pl.BlockSpec((None, d, FFN_TF), lambda c, f, ce, cv: (ce[c], 0, f_idx(c, f, cv))),
            pl.BlockSpec((None, d, FFN_TF), lambda c, f, ce, cv: (ce[c], 0, f_idx(c, f, cv))),
            pl.BlockSpec((None, FFN_TF, d), lambda c, f, ce, cv: (ce[c], f_idx(c, f, cv), 0)),
        ],
        out_specs=pl.BlockSpec((FFN_ROWS, d), lambda c, f, ce, cv: (c, 0), pipeline_mode=pl.Buffered(1)),
    )
    return pl.pallas_call(
        _swiglu_kernel,
        out_shape=jax.ShapeDtypeStruct((rows, d), jnp.float32),
        grid_spec=grid_spec,
        compiler_params=_cparams("parallel", "arbitrary"),
    )(chunk_expert, chunk_valid, xs, wg, wu, wd)


def dense_swiglu(h, wg, wu, wd):
    m, d = h.shape
    n_chunks = pl.cdiv(m, FFN_ROWS)
    pad = n_chunks * FFN_ROWS - m
    xs = jnp.pad(h, ((0, pad), (0, 0)))
    starts = jnp.arange(n_chunks, dtype=jnp.int32) * FFN_ROWS
    valid = (jnp.clip(m - starts, 0, FFN_ROWS) + FFN_SUB - 1) // FFN_SUB
    out = swiglu_chunks(xs, wg[None], wu[None], wd[None], jnp.zeros((n_chunks,), jnp.int32),
                        valid.astype(jnp.int32))
    return out[:m]


def moe_swiglu_rows(h, logits, wg, wu, wd):
    m, d = h.shape
    top_v, top_i = lax.top_k(logits, TOP_K)
    top_w = jax.nn.softmax(top_v, axis=-1)
    n_pairs = m * TOP_K
    max_chunks = (n_pairs + N_EXPERTS * (FFN_ROWS - 1)) // FFN_ROWS
    flat_e = top_i.reshape(-1).astype(jnp.int32)
    order = jnp.argsort(flat_e, stable=True).astype(jnp.int32)
    counts = jnp.sum(jax.nn.one_hot(flat_e, N_EXPERTS, dtype=jnp.int32), axis=0)
    chunks_e = (counts + FFN_ROWS - 1) // FFN_ROWS
    chunk_end = jnp.cumsum(chunks_e)
    chunk_start = chunk_end - chunks_e
    pair_start = jnp.cumsum(counts) - counts
    sorted_e = flat_e[order]
    rank = jnp.arange(n_pairs, dtype=jnp.int32) - pair_start[sorted_e]
    dest_sorted = chunk_start[sorted_e] * FFN_ROWS + rank
    src_row = jnp.zeros((max_chunks * FFN_ROWS,), jnp.int32).at[dest_sorted].set(order // TOP_K)
    pos = jnp.zeros((n_pairs,), jnp.int32).at[order].set(dest_sorted).reshape(m, TOP_K)

    cidx = jnp.arange(max_chunks, dtype=jnp.int32)
    total = chunk_end[-1]
    ce = jnp.sum((cidx[:, None] >= chunk_end[None, :]).astype(jnp.int32), axis=1)
    last_e = jnp.sum((total - 1 >= chunk_end).astype(jnp.int32))
    ce = jnp.where(cidx < total, ce, last_e).astype(jnp.int32)
    rows_in = jnp.clip(counts[ce] - (cidx - chunk_start[ce]) * FFN_ROWS, 0, FFN_ROWS)
    cv = jnp.where(cidx < total, (rows_in + FFN_SUB - 1) // FFN_SUB, 0).astype(jnp.int32)

    xs = jnp.take(h, src_row, axis=0)
    ys = swiglu_chunks(xs, wg, wu, wd, ce, cv)
    y2 = jnp.take(ys, pos.reshape(-1), axis=0).reshape(m, TOP_K, d)
    return y2[:, 0] * top_w[:, 0:1] + y2[:, 1] * top_w[:, 1:2]


def _conv_rows_into(xp_ref, cur_ref, halo_ref, prev_ref, first_chunk, q_len):
    @pl.when(first_chunk)
    def _():
        xp_ref[0:SUBLANES, :] = prev_ref[0]

    @pl.when(jnp.logical_not(first_chunk))
    def _():
        xp_ref[0:SUBLANES, :] = halo_ref[...]

    xp_ref[SUBLANES:SUBLANES + q_len, :] = cur_ref[...]


def _conv_tile(xp_ref, w_ref, lanes, q_len):
    base = SUBLANES - (CONV_W - 1)
    acc = xp_ref[base:base + q_len, lanes] * w_ref[0:1, lanes]
    for j in range(1, CONV_W):
        acc = acc + xp_ref[base + j:base + j + q_len, lanes] * w_ref[j:j + 1, lanes]
    return acc


def _chunk_specs(row_block0, n_chunk, q_len, cols):
    per8 = q_len // SUBLANES

    def cur(s, c):
        return (row_block0 + s * n_chunk + c, 0)

    def halo(s, c):
        return (jnp.maximum((row_block0 + s * n_chunk + c) * per8 - 1, 0), 0)

    return (pl.BlockSpec((q_len, cols), cur), pl.BlockSpec((SUBLANES, cols), halo),
            pl.BlockSpec((1, SUBLANES, cols), lambda s, c: (s, 0, 0)))


def _row_chunk_spec(row_block0, n_chunk, q_len, cols):
    return pl.BlockSpec((q_len, cols), lambda s, c: (row_block0 + s * n_chunk + c, 0))


def _per_chunk_spec(n_chunk, shape):
    nd = len(shape)
    return pl.BlockSpec((1,) + tuple(shape), lambda s, c: (s * n_chunk + c,) + (0,) * nd)


def _per_seq_spec(shape):
    nd = len(shape)
    return pl.BlockSpec((1,) + tuple(shape), lambda s, c: (s,) + (0,) * nd)


def _interleaved(step_generators):
    live = list(step_generators)
    while live:
        live = [gen for gen in live if next(gen, _DONE) is not _DONE]


_DONE = object()


def _iota2(shape):
    return lax.broadcasted_iota(jnp.int32, shape, 0), lax.broadcasted_iota(jnp.int32, shape, 1)


def _gdn_prep_kernel(cur_ref, halo_ref, prev_ref, w_ref, braw_ref, araw_ref, alog_ref, dtb_ref,
                     qh_ref, kh_ref, vh_ref, gt_ref, bt_ref, xp_ref, *, q_len, n_qk, n_v):
    _conv_rows_into(xp_ref, cur_ref, halo_ref, prev_ref, pl.program_id(1) == 0, q_len)

    def tile(t):
        return _silu(_conv_tile(xp_ref, w_ref, slice(t * LANES, (t + 1) * LANES), q_len))

    def l2n(x):
        return x * lax.rsqrt(jnp.sum(x * x, axis=-1, keepdims=True) + 1e-6)

    for h in range(n_qk):
        qh_ref[0, h] = l2n(tile(h)) * (GDN_DK ** -0.5)
    for h in range(n_qk):
        kh_ref[0, h] = l2n(tile(n_qk + h))
    for h in range(n_v):
        vh_ref[0, h] = tile(2 * n_qk + h)

    g = -jnp.exp(alog_ref[...]) * _softplus(araw_ref[...] + dtb_ref[...])
    beta = jax.nn.sigmoid(braw_ref[...])
    ri, ci = _iota2((q_len, q_len))
    gt_ref[0] = _dot_tn(g, (ri <= ci).astype(jnp.float32))
    bt_ref[0] = _dot_tn(beta, (ri == ci).astype(jnp.float32))


def _unit_lower_inverse(a, q_len, ri, ci, eye_f):
    blk = 16
    if q_len <= blk:
        a_d, a_o = a, None
    else:
        same = (ri // blk) == (ci // blk)
        a_d = jnp.where(same, a, 0.0)
        a_o = a - a_d
    p = eye_f + a_d
    x = a_d
    for _ in range(int(math.log2(min(q_len, blk))) - 1):
        x = _dot(x, x)
        p = p + _dot(p, x)
    if a_o is None:
        return p
    y = _dot(p, a_o)
    m = eye_f + y
    for _ in range(int(math.log2(q_len // blk)) - 1):
        y = _dot(y, y)
        m = m + _dot(m, y)
    return _dot(m, p)


def _gdn_chunk_kernel(qh_ref, kh_ref, vh_ref, gt_ref, bt_ref, z_ref, nw_ref, s0_ref, o_ref, s_ref, o_scr,
                      *, q_len, n_qk, rep):
    @pl.when(pl.program_id(1) == 0)
    def _():
        s_ref[...] = s0_ref[...]

    ri, ci = _iota2((q_len, q_len))
    causal = ri >= ci
    strict = ri > ci
    eye = ri == ci
    eye_f = eye.astype(jnp.float32)

    def qk_head(hq, carry):
        q = qh_ref[0, hq]
        k = kh_ref[0, hq]
        kk = _dot_nt(k, k)
        qk = _dot_nt(q, k)
        for r in range(rep):
            h = hq * rep + r
            g_row = gt_ref[0, pl.ds(h, 1), :]
            b_row = bt_ref[0, pl.ds(h, 1), :]
            g_col = jnp.sum(jnp.where(eye, g_row, 0.0), axis=1, keepdims=True)
            b_col = jnp.sum(jnp.where(eye, b_row, 0.0), axis=1, keepdims=True)
            decay = jnp.where(causal, jnp.exp(jnp.where(causal, g_col - g_row, 0.0)), 0.0)
            a = jnp.where(strict, -(b_col * kk) * decay, 0.0)
            t_inv = _unit_lower_inverse(a, q_len, ri, ci, eye_f)
            v = vh_ref[0, h]
            e_col = jnp.exp(g_col)
            uu = _dot(t_inv, v * b_col)
            ww = _dot(t_inv, k * (b_col * e_col))
            s = s_ref[0, h]
            v_new = uu - _dot(ww, s)
            attn = jnp.where(causal, qk * decay, 0.0)
            o = _dot(q * e_col, s) + _dot(attn, v_new)
            g_last = g_row[:, q_len - 1:q_len]
            s_ref[0, h] = s * jnp.exp(g_last) + _dot_tn(k * jnp.exp(g_last - g_col), v_new)
            o_scr[h] = _rms(o, nw_ref[...])
        return carry

    lax.fori_loop(0, n_qk, qk_head, 0)
    for h in range(n_qk * rep):
        lanes = slice(h * LANES, (h + 1) * LANES)
        o_ref[:, lanes] = (o_scr[h] * _silu(z_ref[:, lanes])).astype(o_ref.dtype)


def _gdn_chunk_packed_kernel(qh_ref, kh_ref, vh_ref, gt4_ref, gx_ref, bx_ref, z_ref, nw_ref, s0_ref, o_ref, s_ref,
                             sbd_ref, *, q_len, n_v, rep, n_chunk):
    f32 = jnp.float32
    pack = GDN_PACK
    w = pack * q_len
    wv = pack * GDN_DV
    n_grp = n_v // pack
    qk_per = pack // rep
    c = pl.program_id(1)

    r_w, c_w = _iota2((w, w))
    bd_w = (r_w // q_len) == (c_w // q_len)
    r_v, c_v = _iota2((w, wv))
    bd_v = (r_v // q_len) == (c_v // GDN_DV)
    r_s, c_s = _iota2((wv, wv))
    bd_s = (r_s // GDN_DK) == (c_s // GDN_DV)
    r_k, c_k = _iota2((w, qk_per * GDN_DK))
    bd_k = (r_k // (q_len * rep)) == (c_k // GDN_DK)
    ri, li = _iota2((q_len, w))
    pos = jnp.bitwise_and(li, q_len - 1)
    causal = ri >= pos
    strict = ri > pos
    eye_f = (ri == pos).astype(f32)
    same_blk = (ri // 16) == (pos // 16)

    @pl.when(c == 0)
    def _():
        zero = jnp.zeros((GDN_DK, GDN_DV), f32)
        for g in range(n_grp):
            rows = [jnp.concatenate([s0_ref[0, g * pack + i] if j == i else zero for j in range(pack)], axis=1)
                    for i in range(pack)]
            sbd_ref[g] = jnp.concatenate(rows, axis=0)

    bf16 = jnp.bfloat16
    mask_w = bd_w.astype(f32).astype(bf16)
    mask_v = bd_v.astype(f32).astype(bf16)

    def bd(y16):
        return jnp.concatenate([y16] * pack, axis=0) * mask_w

    def bdv(x16):
        return jnp.concatenate([x16] * pack, axis=0) * mask_v

    def split(x):
        hi = x.astype(bf16)
        return hi, (x - hi.astype(f32)).astype(bf16)

    def dot3(x, y, expand):
        xh, xl = split(x)
        yh, yl = split(y)
        rows = x.shape[0]
        top = jnp.dot(jnp.concatenate([xh, xl], axis=0), expand(yh), preferred_element_type=f32)
        return top[:rows] + top[rows:] + jnp.dot(xh, expand(yl), preferred_element_type=f32)

    def pdot(x, y):
        return dot3(x, y, bd)

    def widen(x):
        return jnp.concatenate([jnp.broadcast_to(x[:, i * q_len:i * q_len + 1], (q_len, GDN_DV))
                                for i in range(pack)], axis=1)

    def group_steps(g):
        lanes_w = pl.ds(pl.multiple_of(g * w, w), w)
        lanes_v = pl.ds(pl.multiple_of(g * wv, wv), wv)
        qs = [qh_ref[0, g * qk_per + j] for j in range(qk_per)]
        ks = [kh_ref[0, g * qk_per + j] for j in range(qk_per)]
        k4 = jnp.concatenate([ks[i // rep] for i in range(pack)], axis=1)
        q4 = jnp.concatenate([qs[i // rep] for i in range(pack)], axis=1)
        v4 = jnp.concatenate([vh_ref[0, g * pack + i] for i in range(pack)], axis=1)
        g_row = gt4_ref[0, pl.ds(g, 1), :]
        g_col = gx_ref[0, :, lanes_w]
        b_col = bx_ref[0, :, lanes_w]
        g_col_v = widen(g_col)
        b_col_v = widen(b_col)
        kcat = jnp.concatenate(ks, axis=1)
        kq = jnp.concatenate([kcat, jnp.concatenate(qs, axis=1)], axis=0)
        k_bd = jnp.where(bd_k, jnp.concatenate([kcat] * pack, axis=0), 0.0)
        kkqk = lax.dot_general(kq.astype(bf16), k_bd.astype(bf16), (((1,), (1,)), ((), ())),
                               preferred_element_type=f32)
        yield
        decay = jnp.where(causal, jnp.exp(jnp.where(causal, g_col - g_row, 0.0)), 0.0)
        a = jnp.where(strict, -(b_col * kkqk[:q_len]) * decay, 0.0)
        a_d = jnp.where(same_blk, a, 0.0)
        p = eye_f + a_d
        x = a_d
        for _ in range(3):
            x = pdot(x, x)
            yield
            p = p + pdot(p, x)
            yield
        y = pdot(p, a - a_d)
        yield
        m = eye_f + y
        y = pdot(y, y)
        yield
        m = m + pdot(m, y)
        yield
        t_inv = pdot(m, p)
        yield
        e_v = jnp.exp(g_col_v)
        uu = dot3(t_inv, v4 * b_col_v, bdv)
        yield
        ww = dot3(t_inv, k4 * (b_col_v * e_v), bdv)
        yield
        s_bd = sbd_ref[g]
        r = jnp.dot(jnp.concatenate([ww, q4 * e_v], axis=0).astype(bf16), s_bd.astype(bf16),
                    preferred_element_type=f32)
        yield
        v_new = uu - r[:q_len]
        v_new16 = v_new.astype(bf16)
        attn = jnp.where(causal, kkqk[q_len:] * decay, 0.0)
        o4 = r[q_len:] + jnp.dot(attn.astype(bf16), bdv(v_new16), preferred_element_type=f32)
        yield
        g_last = g_col_v[q_len - 1:q_len, :]
        upd = lax.dot_general((k4 * jnp.exp(g_last - g_col_v)).astype(bf16), v_new16, (((0,), (0,)), ((), ())),
                              preferred_element_type=f32)
        yield
        s_new = s_bd * jnp.exp(g_last) + jnp.where(bd_s, upd, 0.0)
        o_n = jnp.concatenate([_rms(o4[:, i * GDN_DV:(i + 1) * GDN_DV], nw_ref[...]) for i in range(pack)], axis=1)
        o_out = (o_n * _silu(z_ref[:, lanes_v])).astype(o_ref.dtype)
        yield
        sbd_ref[g] = s_new
        o_ref[:, lanes_v] = o_out

    def several_groups(i, carry):
        _interleaved([group_steps(i * GDN_INTERLEAVE + j) for j in range(GDN_INTERLEAVE)])
        return carry

    lax.fori_loop(0, n_grp // GDN_INTERLEAVE, several_groups, 0)

    @pl.when(c == n_chunk - 1)
    def _():
        for g in range(n_grp):
            for i in range(pack):
                s_ref[0, g * pack + i] = sbd_ref[g, i * GDN_DK:(i + 1) * GDN_DK, i * GDN_DV:(i + 1) * GDN_DV]


def gdn_group(qkv, z, b_raw, a_raw, prev8, s0, p, *, row0, n_seq, n_chunk, q_len):
    n_qk, n_v = GDN_QK_HEADS, GDN_V_HEADS
    cdim = qkv.shape[1]
    nc = n_seq * n_chunk
    rb0 = row0 // q_len
    grid = (n_seq, n_chunk)
    cur_spec, halo_spec, prev_spec = _chunk_specs(rb0, n_chunk, q_len, cdim)
    small = _row_chunk_spec(rb0, n_chunk, q_len, n_v)
    f32 = jnp.float32
    qh, kh, vh, gt, bt = pl.pallas_call(
        functools.partial(_gdn_prep_kernel, q_len=q_len, n_qk=n_qk, n_v=n_v),
        out_shape=(jax.ShapeDtypeStruct((nc, n_qk, q_len, GDN_DK), f32),
                   jax.ShapeDtypeStruct((nc, n_qk, q_len, GDN_DK), f32),
                   jax.ShapeDtypeStruct((nc, n_v, q_len, GDN_DV), f32),
                   jax.ShapeDtypeStruct((nc, n_v, q_len), f32),
                   jax.ShapeDtypeStruct((nc, n_v, q_len), f32)),
        grid=grid,
        in_specs=[cur_spec, halo_spec, prev_spec, _full_spec((CONV_W, cdim)), small, small,
                  _full_spec((1, n_v)), _full_spec((1, n_v))],
        out_specs=(_per_chunk_spec(n_chunk, (n_qk, q_len, GDN_DK)), _per_chunk_spec(n_chunk, (n_qk, q_len, GDN_DK)),
                   _per_chunk_spec(n_chunk, (n_v, q_len, GDN_DV)), _per_chunk_spec(n_chunk, (n_v, q_len)),
                   _per_chunk_spec(n_chunk, (n_v, q_len))),
        scratch_shapes=[pltpu.VMEM((SUBLANES + q_len, cdim), f32)],
        compiler_params=_cparams("parallel", "arbitrary"),
    )(qkv, qkv, prev8, p['gdn_conv_w'], b_raw, a_raw, p['gdn_a_log'].reshape(1, n_v),
      p['gdn_dt_bias'].reshape(1, n_v))

    vdim = n_v * GDN_DV
    if q_len * GDN_PACK == MXU_COLS:
        w = GDN_PACK * q_len
        n_grp = n_v // GDN_PACK
        assert n_v % (GDN_PACK * GDN_INTERLEAVE) == 0
        gt4 = gt.reshape(nc, n_grp, w)
        gx = jnp.repeat(jnp.swapaxes(gt, 1, 2), q_len, axis=2)
        bx = jnp.repeat(jnp.swapaxes(bt, 1, 2), q_len, axis=2)
        state_spec = pl.BlockSpec((1, n_v, GDN_DK, GDN_DV), lambda s, c: (s, 0, 0, 0), pipeline_mode=pl.Buffered(1))
        return pl.pallas_call(
            functools.partial(_gdn_chunk_packed_kernel, q_len=q_len, n_v=n_v, rep=n_v // n_qk, n_chunk=n_chunk),
            out_shape=(jax.ShapeDtypeStruct((nc * q_len, vdim), jnp.bfloat16),
                       jax.ShapeDtypeStruct((n_seq, n_v, GDN_DK, GDN_DV), f32)),
            grid=grid,
            in_specs=[_per_chunk_spec(n_chunk, (n_qk, q_len, GDN_DK)), _per_chunk_spec(n_chunk, (n_qk, q_len, GDN_DK)),
                      _per_chunk_spec(n_chunk, (n_v, q_len, GDN_DV)), _per_chunk_spec(n_chunk, (n_grp, w)),
                      _per_chunk_spec(n_chunk, (q_len, n_v * q_len)), _per_chunk_spec(n_chunk, (q_len, n_v * q_len)),
                      _row_chunk_spec(rb0, n_chunk, q_len, vdim), _full_spec((1, GDN_DV)), state_spec],
            out_specs=(_row_chunk_spec(0, n_chunk, q_len, vdim), state_spec),
            scratch_shapes=[pltpu.VMEM((n_grp, GDN_PACK * GDN_DK, GDN_PACK * GDN_DV), f32)],
            compiler_params=_cparams("parallel", "arbitrary"),
        )(qh, kh, vh, gt4, gx, bx, z, p['gdn_norm'].reshape(1, GDN_DV), s0)
    o, s_fin = pl.pallas_call(
        functools.partial(_gdn_chunk_kernel, q_len=q_len, n_qk=n_qk, rep=n_v // n_qk),
        out_shape=(jax.ShapeDtypeStruct((nc * q_len, vdim), jnp.bfloat16),
                   jax.ShapeDtypeStruct((n_seq, n_v, GDN_DK, GDN_DV), f32)),
        grid=grid,
        in_specs=[_per_chunk_spec(n_chunk, (n_qk, q_len, GDN_DK)), _per_chunk_spec(n_chunk, (n_qk, q_len, GDN_DK)),
                  _per_chunk_spec(n_chunk, (n_v, q_len, GDN_DV)), _per_chunk_spec(n_chunk, (n_v, q_len)),
                  _per_chunk_spec(n_chunk, (n_v, q_len)), _row_chunk_spec(rb0, n_chunk, q_len, vdim),
                  _full_spec((1, GDN_DV)), _per_seq_spec((n_v, GDN_DK, GDN_DV))],
        out_specs=(_row_chunk_spec(0, n_chunk, q_len, vdim), _per_seq_spec((n_v, GDN_DK, GDN_DV))),
        scratch_shapes=[pltpu.VMEM((n_v, q_len, GDN_DV), f32)],
        compiler_params=_cparams("parallel", "arbitrary"),
    )(qh, kh, vh, gt, bt, z, p['gdn_norm'].reshape(1, GDN_DV), s0)
    return o, s_fin


def _ssd_prep_kernel(cur_ref, halo_ref, prev_ref, w_ref, b_ref, dtraw_ref, dtb_ref, alog_ref,
                     xh_ref, bh_ref, ch_ref, cum_ref, dt_ref, cumt_ref, dtt_ref, xp_ref,
                     *, q_len, n_pair, n_grp):
    _conv_rows_into(xp_ref, cur_ref, halo_ref, prev_ref, pl.program_id(1) == 0, q_len)

    def tile(t):
        lanes = slice(t * LANES, (t + 1) * LANES)
        return _silu(_conv_tile(xp_ref, w_ref, lanes, q_len) + b_ref[0:1, lanes])

    for t in range(n_pair):
        xh_ref[0, t] = tile(t)
    for t in range(n_grp):
        bh_ref[0, t] = tile(n_pair + t)
    for t in range(n_grp):
        ch_ref[0, t] = tile(n_pair + n_grp + t)

    dt = _softplus(dtraw_ref[...] + dtb_ref[...])
    dta = dt * (-jnp.exp(alog_ref[...]))
    ri, ci = _iota2((q_len, q_len))
    cum_ref[0] = _dot((ri >= ci).astype(jnp.float32), dta)
    dt_ref[0] = dt
    r2, c2 = _iota2((q_len, 2 * q_len))
    c2 = jnp.where(c2 >= q_len, c2 - q_len, c2)
    cumt_ref[0] = _dot_tn(dta, (r2 <= c2).astype(jnp.float32))
    dtt_ref[0] = _dot_tn(dt, (r2 == c2).astype(jnp.float32))


def _ssd_chunk_kernel(xh_ref, bh_ref, ch_ref, cum_ref, dt_ref, cumt_ref, dtt_ref, z_ref, d_ref, nw_ref, s0_ref,
                      o_ref, s_ref, y_scr, *, q_len, n_pair, n_grp):
    @pl.when(pl.program_id(1) == 0)
    def _():
        s_ref[...] = s0_ref[...]

    n_head = 2 * n_pair
    pairs_per_grp = n_pair // n_grp
    half = LANES // 2
    r2, c2 = _iota2((q_len, 2 * q_len))
    second2 = c2 >= q_len
    causal2 = r2 >= jnp.where(second2, c2 - q_len, c2)
    lane_row2 = lax.broadcasted_iota(jnp.int32, (1, 2 * q_len), 1) >= q_len
    hx, lx = _iota2((n_head, LANES))
    h2, l2 = _iota2((n_head, 2 * q_len))
    rx, cx = _iota2((LANES, LANES))
    eye_x = rx == cx
    lane_x = lax.broadcasted_iota(jnp.int32, (1, LANES), 1)
    cum = cum_ref[0]
    dt = dt_ref[0]

    def pair_steps(p):
        g = p // pairs_per_grp
        h0 = 2 * p
        x = xh_ref[0, p]
        bg = bh_ref[0, g]
        cg = ch_ref[0, g]
        s = s_ref[0, p]
        cb2 = _bdot_nt(cg, jnp.concatenate([bg, bg], axis=0))
        yield
        crow2 = jnp.where(lane_row2, cumt_ref[0, pl.ds(h0 + 1, 1), :], cumt_ref[0, pl.ds(h0, 1), :])
        dtrow2 = jnp.where(lane_row2, dtt_ref[0, pl.ds(h0 + 1, 1), :], dtt_ref[0, pl.ds(h0, 1), :])
        sel2 = (h2 == h0 + (l2 >= q_len).astype(jnp.int32)).astype(jnp.float32)
        selx = (hx == h0 + (lx >= half).astype(jnp.int32)).astype(jnp.float32)
        ccol2 = _dot(cum, sel2)
        yield
        ccolx = _dot(cum, selx)
        yield
        dtcolx = _dot(dt, selx)
        yield
        m = jnp.where(causal2, cb2 * jnp.exp(jnp.where(causal2, ccol2 - crow2, 0.0)) * dtrow2, 0.0)
        xstack = jnp.concatenate([jnp.where(lane_x < half, x, 0.0), jnp.where(lane_x >= half, x, 0.0)], axis=0)
        y = _bdot(m, xstack)
        yield
        y = y + _bdot_nt(cg, s) * jnp.exp(ccolx) + d_ref[0, pl.ds(p, 1), :] * x
        yield
        clastx = ccolx[q_len - 1:q_len, :]
        wx = jnp.exp(clastx - ccolx) * dtcolx
        dec_col = jnp.sum(jnp.where(eye_x, jnp.exp(clastx), 0.0), axis=1, keepdims=True)
        s_new = s * dec_col + _bdot_tn(x * wx, bg)
        yield
        s_ref[0, p] = s_new
        y_scr[p] = y

    def several_pairs(i, carry):
        _interleaved([pair_steps(i * SSD_INTERLEAVE + j) for j in range(SSD_INTERLEAVE)])
        return carry

    lax.fori_loop(0, n_pair // SSD_INTERLEAVE, several_pairs, 0)

    gsz = pairs_per_grp * LANES
    for g in range(n_grp):
        parts = []
        ss = None
        for i in range(pairs_per_grp):
            t = g * pairs_per_grp + i
            lanes = slice(t * LANES, (t + 1) * LANES)
            yz = y_scr[t] * _silu(z_ref[:, lanes])
            parts.append((lanes, yz))
            sq = jnp.sum(yz * yz, axis=-1, keepdims=True)
            ss = sq if ss is None else ss + sq
        scale = lax.rsqrt(ss / gsz + NORM_EPS)
        for lanes, yz in parts:
            o_ref[:, lanes] = ((yz * scale) * nw_ref[0:1, lanes]).astype(o_ref.dtype)


def ssd_group(xbc, z, dt_raw, prev8, s0, p, *, row0, n_seq, n_chunk, q_len):
    n_pair = MB_HEADS // 2
    n_grp = MB_GROUPS
    n_head = MB_HEADS
    cdim = xbc.shape[1]
    nc = n_seq * n_chunk
    rb0 = row0 // q_len
    grid = (n_seq, n_chunk)
    f32 = jnp.float32
    cur_spec, halo_spec, prev_spec = _chunk_specs(rb0, n_chunk, q_len, cdim)
    xh, bh, ch, cum, dt, cumt, dtt = pl.pallas_call(
        functools.partial(_ssd_prep_kernel, q_len=q_len, n_pair=n_pair, n_grp=n_grp),
        out_shape=(jax.ShapeDtypeStruct((nc, n_pair, q_len, LANES), f32),
                   jax.ShapeDtypeStruct((nc, n_grp, q_len, MB_STATE), f32),
                   jax.ShapeDtypeStruct((nc, n_grp, q_len, MB_STATE), f32),
                   jax.ShapeDtypeStruct((nc, q_len, n_head), f32),
                   jax.ShapeDtypeStruct((nc, q_len, n_head), f32),
                   jax.ShapeDtypeStruct((nc, n_head, 2 * q_len), f32),
                   jax.ShapeDtypeStruct((nc, n_head, 2 * q_len), f32)),
        grid=grid,
        in_specs=[cur_spec, halo_spec, prev_spec, _full_spec((CONV_W, cdim)), _full_spec((1, cdim)),
                  _row_chunk_spec(rb0, n_chunk, q_len, n_head), _full_spec((1, n_head)), _full_spec((1, n_head))],
        out_specs=(_per_chunk_spec(n_chunk, (n_pair, q_len, LANES)), _per_chunk_spec(n_chunk, (n_grp, q_len, MB_STATE)),
                   _per_chunk_spec(n_chunk, (n_grp, q_len, MB_STATE)), _per_chunk_spec(n_chunk, (q_len, n_head)),
                   _per_chunk_spec(n_chunk, (q_len, n_head)), _per_chunk_spec(n_chunk, (n_head, 2 * q_len)),
                   _per_chunk_spec(n_chunk, (n_head, 2 * q_len))),
        scratch_shapes=[pltpu.VMEM((SUBLANES + q_len, cdim), f32)],
        compiler_params=_cparams("parallel", "arbitrary"),
    )(xbc, xbc, prev8, p['mb_conv_w'], p['mb_conv_b'].reshape(1, cdim), dt_raw,
      p['mb_dt_bias'].reshape(1, n_head), p['mb_a_log'].reshape(1, n_head))

    d_lanes = jnp.repeat(p['mb_d'], MB_HEADDIM).reshape(1, n_pair, LANES)
    y, s_fin = pl.pallas_call(
        functools.partial(_ssd_chunk_kernel, q_len=q_len, n_pair=n_pair, n_grp=n_grp),
        out_shape=(jax.ShapeDtypeStruct((nc * q_len, MB_DIM), jnp.bfloat16),
                   jax.ShapeDtypeStruct((n_seq, n_pair, LANES, MB_STATE), f32)),
        grid=grid,
        in_specs=[_per_chunk_spec(n_chunk, (n_pair, q_len, LANES)), _per_chunk_spec(n_chunk, (n_grp, q_len, MB_STATE)),
                  _per_chunk_spec(n_chunk, (n_grp, q_len, MB_STATE)), _per_chunk_spec(n_chunk, (q_len, n_head)),
                  _per_chunk_spec(n_chunk, (q_len, n_head)), _per_chunk_spec(n_chunk, (n_head, 2 * q_len)),
                  _per_chunk_spec(n_chunk, (n_head, 2 * q_len)), _row_chunk_spec(rb0, n_chunk, q_len, MB_DIM),
                  _full_spec((1, n_pair, LANES)), _full_spec((1, MB_DIM)), _per_seq_spec((n_pair, LANES, MB_STATE))],
        out_specs=(_row_chunk_spec(0, n_chunk, q_len, MB_DIM), _per_seq_spec((n_pair, LANES, MB_STATE))),
        scratch_shapes=[pltpu.VMEM((n_pair, q_len, LANES), f32)],
        compiler_params=_cparams("parallel", "arbitrary"),
    )(xh, bh, ch, cum, dt, cumt, dtt, z, d_lanes, p['mb_norm'].reshape(1, MB_DIM), s0)
    return y, s_fin


def _s5_step(h_re, h_im, a_re, a_im, bu_re, bu_im):
    return a_re * h_re - a_im * h_im + bu_re, a_re * h_im + a_im * h_re + bu_im


def _s5_pass1_kernel(u_ref, bre_ref, bim_ref, are_ref, aim_ref, vre_ref, vim_ref):
    a_re, a_im = are_ref[...], aim_ref[...]
    b_re, b_im = bre_ref[0].astype(jnp.bfloat16), bim_ref[0].astype(jnp.bfloat16)
    h_re = _bdot(u_ref[0], b_re)
    h_im = _bdot(u_ref[0], b_im)
    for t in range(1, S5_SUB):
        u = u_ref[t].astype(jnp.bfloat16)
        h_re, h_im = _s5_step(h_re, h_im, a_re, a_im, _bdot(u, b_re), _bdot(u, b_im))
    vre_ref[...] = h_re
    vim_ref[...] = h_im


def _s5_pass2_kernel(u_ref, hre_ref, him_ref, bre_ref, bim_ref, cre_ref, cim_ref, are_ref, aim_ref, d_ref, y_ref):
    a_re, a_im = are_ref[...], aim_ref[...]
    b_re, b_im = bre_ref[0].astype(jnp.bfloat16), bim_ref[0].astype(jnp.bfloat16)
    c_re, c_im = cre_ref[0].astype(jnp.bfloat16), cim_ref[0].astype(jnp.bfloat16)
    h_re, h_im = hre_ref[...], him_ref[...]
    for t in range(S5_SUB):
        u = u_ref[t]
        u16 = u.astype(jnp.bfloat16)
        h_re, h_im = _s5_step(h_re, h_im, a_re, a_im, _bdot(u16, b_re), _bdot(u16, b_im))
        y_ref[t] = _bdot(h_re, c_re) - _bdot(h_im, c_im) + d_ref[...] * u


def _s5_carry_kernel(vre_ref, vim_ref, are_ref, aim_ref, s0re_ref, s0im_ref,
                     hre_ref, him_ref, pre_ref, pim_ref, sre_ref, sim_ref, *, n_main, n_dec):
    a_re, a_im = are_ref[...], aim_ref[...]
    half = SUBLANES // 2
    meta = pl.ds(n_main, SUBLANES)
    row = lax.broadcasted_iota(jnp.int32, (SUBLANES, vre_ref.shape[1]), 0)
    first_half = row < half

    def cmul_add(h_re, h_im, v_re, v_im):
        return a_re * h_re - a_im * h_im + v_re, a_re * h_im + a_im * h_re + v_im

    def two_pieces(i, carry):
        h_re, h_im = carry
        rows = pl.ds(pl.multiple_of(i * SUBLANES, SUBLANES), SUBLANES)
        v_re, v_im = vre_ref[rows, :], vim_ref[rows, :]
        t_re, t_im = cmul_add(h_re, h_im, v_re, v_im)
        t_re = pltpu.roll(t_re, half, 0)
        t_im = pltpu.roll(t_im, half, 0)
        hre_ref[rows, :] = jnp.where(first_half, h_re, t_re)
        him_ref[rows, :] = jnp.where(first_half, h_im, t_im)
        n_re, n_im = cmul_add(t_re, t_im, v_re, v_im)
        return pltpu.roll(n_re, half, 0), pltpu.roll(n_im, half, 0)

    h_re, h_im = lax.fori_loop(0, n_main // SUBLANES, two_pieces, (vre_ref[meta, :], vim_ref[meta, :]))
    pre_ref[...] = h_re
    pim_ref[...] = h_im
    hre_ref[meta, :] = jnp.zeros((SUBLANES, vre_ref.shape[1]), jnp.float32)
    him_ref[meta, :] = jnp.zeros((SUBLANES, vre_ref.shape[1]), jnp.float32)
    dec = pl.ds(n_main + SUBLANES, n_dec)
    s0_re, s0_im = s0re_ref[...], s0im_ref[...]
    hre_ref[dec, :] = s0_re
    him_ref[dec, :] = s0_im
    f_re, f_im = cmul_add(s0_re, s0_im, vre_ref[dec, :], vim_ref[dec, :])
    sre_ref[...] = f_re
    sim_ref[...] = f_im
    tail = pl.ds(n_main + SUBLANES + n_dec, SUBLANES)
    hre_ref[tail, :] = jnp.zeros((SUBLANES, vre_ref.shape[1]), jnp.float32)
    him_ref[tail, :] = jnp.zeros((SUBLANES, vre_ref.shape[1]), jnp.float32)


def _s5_post_kernel(y_ref, w_ref, b_ref, nw_ref, o_ref, wbf_ref):
    @pl.when(pl.program_id(0) == 0)
    def _():
        wbf_ref[...] = w_ref[...].astype(jnp.bfloat16)

    gl = jax.nn.gelu(y_ref[...])
    gate = jnp.dot(gl.astype(jnp.bfloat16), wbf_ref[...], preferred_element_type=jnp.float32) + b_ref[...]
    o_ref[...] = _rms(gl * jax.nn.sigmoid(gate), nw_ref[...]).astype(o_ref.dtype)


def _s5_discretize(p):
    lam_re, lam_im = p['s5_lambda_re'], p['s5_lambda_im']
    step = jnp.exp(p['s5_log_step'])[:, None]
    mag = jnp.exp(lam_re * step)
    a_re = mag * jnp.cos(lam_im * step)
    a_im = mag * jnp.sin(lam_im * step)
    den = lam_re * lam_re + lam_im * lam_im
    f_re = ((a_re - 1.0) * lam_re + a_im * lam_im) / den
    f_im = (a_im * lam_re - (a_re - 1.0) * lam_im) / den
    bb_re = f_re[..., None] * p['s5_b_re'] - f_im[..., None] * p['s5_b_im']
    bb_im = f_re[..., None] * p['s5_b_im'] + f_im[..., None] * p['s5_b_re']
    return a_re, a_im, bb_re, bb_im


def _block_diag(w):
    g, r, c = w.shape
    nb = g // S5_GB
    eye = jnp.eye(S5_GB, dtype=w.dtype)
    return jnp.einsum('bgrc,gh->bgrhc', w.reshape(nb, S5_GB, r, c), eye).reshape(nb, S5_GB * r, S5_GB * c)


def s5_mixer_rows(u, s0_re, s0_im, p, *, n_prompt, seq, n_dec):
    f32 = jnp.float32
    n_main = n_prompt * seq
    n_meta = n_prompt * N_META
    assert n_prompt == SUBLANES // 2 and seq % (2 * S5_SUB) == 0 and n_dec % SUBLANES == 0
    pieces = seq // S5_SUB
    n_main_p = pieces * n_prompt
    n_piece = n_main_p + SUBLANES + n_dec + SUBLANES
    assert n_piece % S5_ROWS == 0
    a_re, a_im, bb_re, bb_im = _s5_discretize(p)
    sdim = S5_GROUPS * S5_STATE
    nb = S5_GROUPS // S5_GB
    gw = S5_GB * S5_STATE
    uw = S5_GB * S5_GROUP
    b_re = _block_diag(jnp.swapaxes(bb_re, 1, 2))
    b_im = _block_diag(jnp.swapaxes(bb_im, 1, 2))
    c_re = _block_diag(jnp.swapaxes(p['s5_c_re'], 1, 2))
    c_im = _block_diag(jnp.swapaxes(p['s5_c_im'], 1, 2))
    a16_re, a16_im = a_re, a_im
    for _ in range(int(math.log2(S5_SUB))):
        a16_re, a16_im = a16_re * a16_re - a16_im * a16_im, 2.0 * a16_re * a16_im
    flat = lambda a: a.reshape(1, sdim)

    um = u[:n_main].reshape(n_prompt, pieces, S5_SUB, S5_DIM).transpose(2, 1, 0, 3).reshape(S5_SUB, n_main_p, S5_DIM)
    ue = u[n_main:n_main + n_meta].reshape(n_prompt, S5_SUB, S5_DIM).transpose(1, 0, 2)
    us = u[n_main + n_meta:].reshape(n_dec, S5_SUB, S5_DIM).transpose(1, 0, 2)
    zpad = jnp.zeros((S5_SUB, SUBLANES - n_prompt, S5_DIM), f32)
    u4 = jnp.concatenate([um, ue, zpad, us, jnp.zeros((S5_SUB, SUBLANES, S5_DIM), f32)], axis=1)

    grid = (nb, n_piece // S5_ROWS)
    u_spec = pl.BlockSpec((S5_SUB, S5_ROWS, uw), lambda b, r: (0, r, b))
    st_spec = pl.BlockSpec((S5_ROWS, gw), lambda b, r: (r, b))
    bw_spec = pl.BlockSpec((1, uw, gw), lambda b, r: (b, 0, 0))
    cw_spec = pl.BlockSpec((1, gw, uw), lambda b, r: (b, 0, 0))
    a_spec = pl.BlockSpec((1, gw), lambda b, r: (0, b))
    v_re, v_im = pl.pallas_call(
        _s5_pass1_kernel,
        out_shape=(jax.ShapeDtypeStruct((n_piece, sdim), f32), jax.ShapeDtypeStruct((n_piece, sdim), f32)),
        grid=grid, in_specs=[u_spec, bw_spec, bw_spec, a_spec, a_spec], out_specs=(st_spec, st_spec),
        compiler_params=_cparams("parallel", "parallel"),
    )(u4, b_re, b_im, flat(a_re), flat(a_im))

    lanes_blk = 1024
    col = lambda rows: pl.BlockSpec((rows, lanes_blk), lambda j: (0, j))
    h_re, h_im, p_re, p_im, s_re, s_im = pl.pallas_call(
        functools.partial(_s5_carry_kernel, n_main=n_main_p, n_dec=n_dec),
        out_shape=(jax.ShapeDtypeStruct((n_piece, sdim), f32), jax.ShapeDtypeStruct((n_piece, sdim), f32),
                   jax.ShapeDtypeStruct((SUBLANES, sdim), f32), jax.ShapeDtypeStruct((SUBLANES, sdim), f32),
                   jax.ShapeDtypeStruct((n_dec, sdim), f32), jax.ShapeDtypeStruct((n_dec, sdim), f32)),
        grid=(sdim // lanes_blk,),
        in_specs=[col(n_piece), col(n_piece), col(1), col(1), col(n_dec), col(n_dec)],
        out_specs=(col(n_piece), col(n_piece), col(SUBLANES), col(SUBLANES), col(n_dec), col(n_dec)),
        compiler_params=_cparams("parallel"),
    )(v_re, v_im, flat(a16_re), flat(a16_im), s0_re, s0_im)

    y4 = pl.pallas_call(
        _s5_pass2_kernel,
        out_shape=jax.ShapeDtypeStruct((S5_SUB, n_piece, S5_DIM), f32),
        grid=grid,
        in_specs=[u_spec, st_spec, st_spec, bw_spec, bw_spec, cw_spec, cw_spec, a_spec, a_spec,
                  pl.BlockSpec((1, uw), lambda b, r: (0, b))],
        out_specs=u_spec,
        compiler_params=_cparams("parallel", "parallel"),
    )(u4, h_re, h_im, b_re, b_im, c_re, c_im, flat(a_re), flat(a_im), p['s5_d'].reshape(1, S5_DIM))

    ym = y4[:, :n_main_p].reshape(S5_SUB, pieces, n_prompt, S5_DIM).transpose(2, 1, 0, 3).reshape(n_main, S5_DIM)
    ye = y4[:, n_main_p:n_main_p + n_prompt].transpose(1, 0, 2).reshape(n_meta, S5_DIM)
    ys = y4[:, n_main_p + SUBLANES:n_main_p + SUBLANES + n_dec].transpose(1, 0, 2).reshape(n_dec * S5_SUB, S5_DIM)
    y = jnp.concatenate([ym, ye, ys], axis=0)

    m = y.shape[0]
    mixed = pl.pallas_call(
        _s5_post_kernel,
        out_shape=jax.ShapeDtypeStruct((m, S5_DIM), jnp.bfloat16),
        grid=(m // ROW_TILE,),
        in_specs=[_row_spec(S5_DIM),
                  pl.BlockSpec((S5_DIM, S5_DIM), lambda i: (0, 0), pipeline_mode=pl.Buffered(1)),
                  _full_spec((1, S5_DIM)), _full_spec((1, S5_DIM))],
        out_specs=_row_spec(S5_DIM),
        scratch_shapes=[pltpu.VMEM((S5_DIM, S5_DIM), jnp.bfloat16)],
        compiler_params=_cparams("arbitrary"),
    )(y, p['s5_glu_w'], p['s5_glu_b'].reshape(1, S5_DIM), p['s5_norm'].reshape(1, S5_DIM))
    return mixed, p_re[:n_prompt], p_im[:n_prompt], s_re, s_im


def _conv_history(x, cache, n_prompt, seq, n_dec, dec_len):
    n_main = n_prompt * seq
    n_meta = n_prompt * N_META
    k = CONV_W - 1
    cdim = x.shape[1]
    pad = ((0, 0), (SUBLANES - k, 0), (0, 0))
    meta = x[n_main:n_main + n_meta].reshape(n_prompt, N_META, cdim)
    prev_short = jnp.concatenate([jnp.zeros((n_prompt, SUBLANES, cdim), x.dtype), jnp.pad(cache, pad)], axis=0)
    prev_main = jnp.pad(meta[:, N_META - k:], pad)
    new_prompt = x[:n_main].reshape(n_prompt, seq, cdim)[:, seq - k:]
    new_sample = x[n_main + n_meta:].reshape(n_dec, dec_len, cdim)[:, dec_len - k:]
    return prev_short, prev_main, new_prompt, new_sample


def _two_groups(group_fn, prev_short, prev_main, s0_sample, n_prompt, seq, n_dec, dec_len):
    n_main = n_prompt * seq
    s0_short = jnp.concatenate([jnp.zeros((n_prompt,) + s0_sample.shape[1:], s0_sample.dtype), s0_sample], axis=0)
    y_short, s_short = group_fn(prev_short, s0_short, row0=n_main, n_seq=n_prompt + n_dec, n_chunk=1, q_len=N_META)
    y_main, s_main = group_fn(prev_main, s_short[:n_prompt], row0=0, n_seq=n_prompt, n_chunk=seq // CHUNK,
                              q_len=CHUNK)
    return jnp.concatenate([y_main, y_short], axis=0), s_main, s_short[n_prompt:]


def kernel(x_prompt, x_sample, state_s5_re, state_s5_im, state_ssd, cache_ssd_conv, state_gdn, cache_gdn_conv, meta_tokens, norm_mix0, w_in0, s5_lambda_re, s5_lambda_im, s5_log_step, s5_b_re, s5_b_im, s5_c_re, s5_c_im, s5_d, s5_glu_w, s5_glu_b, s5_norm, mb_conv_w, mb_conv_b, mb_dt_bias, mb_a_log, mb_d, mb_norm, w_out0, norm_ffn0, ffn_w_gate, ffn_w_up, ffn_w_down, norm_mix1, w_in1, gdn_conv_w, gdn_a_log, gdn_dt_bias, gdn_norm, w_out1, norm_ffn1, moe_router, moe_w_gate, moe_w_up, moe_w_down, norm_final):
    p = dict(s5_lambda_re=s5_lambda_re, s5_lambda_im=s5_lambda_im, s5_log_step=s5_log_step, s5_b_re=s5_b_re,
             s5_b_im=s5_b_im, s5_c_re=s5_c_re, s5_c_im=s5_c_im, s5_d=s5_d, s5_glu_w=s5_glu_w, s5_glu_b=s5_glu_b,
             s5_norm=s5_norm, mb_conv_w=mb_conv_w, mb_conv_b=mb_conv_b, mb_dt_bias=mb_dt_bias, mb_a_log=mb_a_log,
             mb_d=mb_d, mb_norm=mb_norm, gdn_conv_w=gdn_conv_w, gdn_a_log=gdn_a_log, gdn_dt_bias=gdn_dt_bias,
             gdn_norm=gdn_norm)
    n_prompt, seq, d = x_prompt.shape
    n_dec, dec_len, _ = x_sample.shape
    assert dec_len == N_META and seq % CHUNK == 0
    n_main = n_prompt * seq
    n_meta = n_prompt * N_META
    groups = (n_prompt, seq, n_dec, dec_len)

    x0 = jnp.concatenate([x_prompt.reshape(n_main, d), jnp.tile(meta_tokens, (n_prompt, 1)),
                          x_sample.reshape(n_dec * dec_len, d)], axis=0)

    h = rms_norm_rows(x0, norm_mix0, jnp.bfloat16)
    o1 = S5_DIM
    o2 = o1 + MB_DIM
    o3 = o2 + MB_CONV_DIM
    u = matmul_rows(h, w_in0, col0=0, ncols=S5_DIM)
    z0 = matmul_rows(h, w_in0, col0=o1, ncols=MB_DIM)
    xbc = matmul_rows(h, w_in0, col0=o2, ncols=MB_CONV_DIM)
    dt_raw = matmul_rows(h, w_in0[:, o3:])

    y_s5, p_s5_re, p_s5_im, s_s5_re, s_s5_im = s5_mixer_rows(
        u, state_s5_re.reshape(n_dec, -1), state_s5_im.reshape(n_dec, -1), p, n_prompt=n_prompt, seq=seq, n_dec=n_dec)
    s5_shape = (S5_GROUPS, S5_STATE)

    prev_short, prev_main, p_ssd_conv, s_ssd_conv = _conv_history(xbc, cache_ssd_conv, *groups)
    ssd_fn = functools.partial(ssd_group, xbc, z0, dt_raw)
    y_mb, p_ssd, s_ssd = _two_groups(
        lambda prev8, s0, **kw: ssd_fn(prev8, s0, p, **kw), prev_short, prev_main,
        state_ssd.reshape(n_dec, MB_HEADS // 2, LANES, MB_STATE), *groups)
    ssd_shape = (MB_HEADS, MB_HEADDIM, MB_STATE)

    mixed = jnp.concatenate([y_s5, y_mb], axis=1)
    x1 = matmul_rows(mixed, w_out0, tn=256, residual=x0)
    h = rms_norm_rows(x1, norm_ffn0, jnp.bfloat16)
    ffn = dense_swiglu(h, ffn_w_gate, ffn_w_up, ffn_w_down)

    x2, h = add_rms_norm_rows(x1, ffn, norm_mix1)
    o1 = GDN_CONV_DIM
    o2 = o1 + GDN_V_DIM
    o3 = o2 + GDN_V_HEADS
    qkv = matmul_rows(h, w_in1, col0=0, ncols=GDN_CONV_DIM)
    z1 = matmul_rows(h, w_in1, col0=o1, ncols=GDN_V_DIM)
    b_raw = matmul_rows(h, w_in1[:, o2:o3])
    a_raw = matmul_rows(h, w_in1[:, o3:])
    prev_short, prev_main, p_gdn_conv, s_gdn_conv = _conv_history(qkv, cache_gdn_conv, *groups)
    gdn_fn = functools.partial(gdn_group, qkv, z1, b_raw, a_raw)
    y_gdn, p_gdn, s_gdn = _two_groups(
        lambda prev8, s0, **kw: gdn_fn(prev8, s0, p, **kw), prev_short, prev_main, state_gdn, *groups)

    x3 = matmul_rows(y_gdn, w_out1, tn=256, residual=x2)
    h, logits = rms_norm_router_rows(x3, norm_ffn1, moe_router)
    moe = moe_swiglu_rows(h, logits, moe_w_gate, moe_w_up, moe_w_down)
    y = add_rms_norm_only_rows(x3, moe, norm_final, jnp.float32)

    y_prompt = y[:n_main].reshape(n_prompt, seq, d)
    y_sample = y[n_main + n_meta:].reshape(n_dec, dec_len, d)
    return (y_prompt, y_sample,
            p_s5_re.reshape((n_prompt,) + s5_shape), p_s5_im.reshape((n_prompt,) + s5_shape),
            p_ssd.reshape((n_prompt,) + ssd_shape), p_ssd_conv, p_gdn, p_gdn_conv,
            s_s5_re.reshape((n_dec,) + s5_shape), s_s5_im.reshape((n_dec,) + s5_shape),
            s_ssd.reshape((n_dec,) + ssd_shape), s_ssd_conv, s_gdn, s_gdn_conv)
```
